```python
import jax, jax.numpy as jnp
from jax import lax
import numpy as np

D_MODEL = 1024
BATCH = 8
SEQ = 4096
DEPTH = 4

GRID_W = 64
CTX_LEN = 256
N_MIXERS = 3
EPS = 1e-6
D_FF = -(-8 * D_MODEL // (3 * 256)) * 256
D_RNN = (4 * D_MODEL // 3) // 256 * 256
LRU_BLOCK_W = 128
LRU_BLOCKS = D_RNN // LRU_BLOCK_W
LRU_C = 8.0
CONV_W = 4
CONV_LEFT = 2
NA_HEAD_DIM = 64
NA_HEADS = D_MODEL // NA_HEAD_DIM
WIN_ROWS = 8
WIN_COLS = 16
HG_DK = 128
HG_HEADS = D_MODEL // HG_DK
HG_DV = D_MODEL // HG_HEADS
HG_CHUNK = 64

kernel_name = 'hybrid_rglru_natten_hgrn2_prefix_trunk'


def rms_norm(x, g):
    xf = x.astype(jnp.float32)
    y = xf * lax.rsqrt(jnp.mean(xf * xf, axis=-1, keepdims=True) + EPS)
    return (y * g.astype(jnp.float32)).astype(x.dtype)


def modulate(h, shift, scale):
    return h * (1 + scale) + shift


def swiglu(h, w_gu, w_down):
    a, b = jnp.split(h @ w_gu, 2, axis=-1)
    return (jax.nn.silu(a) * b) @ w_down


def short_conv(u, w, b):
    L = u.shape[1]
    up = jnp.pad(u, ((0, 0), (CONV_LEFT, CONV_W - 1 - CONV_LEFT), (0, 0)))
    out = b + w[0] * up[:, 0:L]
    for j in range(1, CONV_W):
        out = out + w[j] * up[:, j:j + L]
    return out


def linear_scan(a, b, h0):
    b = b.at[:, 0].add(a[:, 0] * h0)

    def combine(e1, e2):
        a1, b1 = e1
        a2, b2 = e2
        return a1 * a2, a2 * b1 + b2

    _, h = lax.associative_scan(combine, (a, b), axis=1)
    return h


def flip_seq(t, d):
    return jnp.flip(t, axis=1) if d == 1 else t


def rglru_gates(u, gate_w, gate_b, lam):
    B, L, _ = u.shape
    ub = u.reshape(B, L, LRU_BLOCKS, LRU_BLOCK_W)
    z = jnp.einsum('blnj,gnjk->gblnk', ub, gate_w).reshape(2, B, L, D_RNN)
    z = (z + gate_b[:, None, None, :]).astype(jnp.float32)
    r = jax.nn.sigmoid(z[0])
    i = jax.nn.sigmoid(z[1])
    log_a = -LRU_C * r * jax.nn.softplus(-lam.astype(jnp.float32))
    a = jnp.exp(log_a)
    b = jnp.sqrt(-jnp.expm1(2.0 * log_a)) * i * u.astype(jnp.float32)
    return a, b


def rglru_mixer(h_ctx, h_lat, w_in, conv_w, conv_b, gate_w, gate_b, lam, w_out, need_ctx_out):
    w_gate, w_rec = w_in[:, :D_RNN], w_in[:, D_RNN:]
    u_ctx = short_conv(h_ctx @ w_rec, conv_w, conv_b)
    u_lat = short_conv(h_lat @ w_rec, conv_w, conv_b)
    h0 = jnp.zeros((h_ctx.shape[0], D_RNN), jnp.float32)
    rec_ctx = jnp.zeros(u_ctx.shape, jnp.float32)
    rec_lat = jnp.zeros(u_lat.shape, jnp.float32)
    for d in range(2):
        hc = linear_scan(*rglru_gates(flip_seq(u_ctx, d), gate_w[d], gate_b[d], lam[d]), h0)
        hx = linear_scan(*rglru_gates(flip_seq(u_lat, d), gate_w[d], gate_b[d], lam[d]), hc[:, -1])
        rec_lat = rec_lat + flip_seq(hx, d)
        if need_ctx_out:
            rec_ctx = rec_ctx + flip_seq(hc, d)

    def readout(h, rec):
        return (rec.astype(h.dtype) * jax.nn.gelu(h @ w_gate)) @ w_out

    y_ctx = readout(h_ctx, rec_ctx) if need_ctx_out else None
    return y_ctx, readout(h_lat, rec_lat)


def na_mixer(h_ctx, h_lat, w_qkv, q_g, k_g, rpb, w_o, need_ctx_out):
    B, L, _ = h_lat.shape
    n_ctx = h_ctx.shape[1]
    rows = L // GRID_W
    kr = min(WIN_ROWS, rows)
    n_loc = kr * WIN_COLS
    scale = NA_HEAD_DIM ** -0.5

    def heads(t):
        return t.reshape(t.shape[0], t.shape[1], NA_HEADS, NA_HEAD_DIM)

    kv_c = h_ctx @ w_qkv[:, D_MODEL:]
    k_c = rms_norm(heads(kv_c[..., :D_MODEL]), k_g)
    v_c = heads(kv_c[..., D_MODEL:])

    def grid(t):
        return t.reshape(B, rows, GRID_W, NA_HEADS, NA_HEAD_DIM)

    q, k, v = jnp.split(h_lat @ w_qkv, 3, axis=-1)
    q = grid(rms_norm(heads(q), q_g))
    k = grid(rms_norm(heads(k), k_g))
    v = grid(heads(v))

    col_start = np.clip(np.arange(GRID_W) - WIN_COLS // 2, 0, GRID_W - WIN_COLS)
    col_idx = col_start[:, None] + np.arange(WIN_COLS)[None, :]
    col_off = col_idx - np.arange(GRID_W)[:, None] + (WIN_COLS - 1)

    def one_row(args):
        r, q_r = args
        rs = jnp.clip(r - kr // 2, 0, rows - kr)
        k_n = lax.dynamic_slice_in_dim(k, rs, kr, axis=1)[:, :, col_idx]
        v_n = lax.dynamic_slice_in_dim(v, rs, kr, axis=1)[:, :, col_idx]
        row_off = rs + jnp.arange(kr) - r + (WIN_ROWS - 1)
        bias = rpb[:, row_off[None, :, None], col_off[:, None, :]]
        s_loc = (jnp.einsum('bqhd,brqkhd->bhqrk', q_r, k_n).astype(jnp.float32) * scale
                 + bias.astype(jnp.float32)[None])
        s_ctx = jnp.einsum('bqhd,bchd->bhqc', q_r, k_c).astype(jnp.float32) * scale
        logits = jnp.concatenate([s_loc.reshape(B, NA_HEADS, GRID_W, n_loc), s_ctx], axis=-1)
        p = jax.nn.softmax(logits, axis=-1).astype(v.dtype)
        p_loc = p[..., :n_loc].reshape(B, NA_HEADS, GRID_W, kr, WIN_COLS)
        return (jnp.einsum('bhqrk,brqkhd->bqhd', p_loc, v_n)
                + jnp.einsum('bhqc,bchd->bqhd', p[..., n_loc:], v_c))

    o = lax.map(one_row, (jnp.arange(rows), jnp.moveaxis(q, 1, 0)))
    y_lat = jnp.moveaxis(o, 0, 1).reshape(B, L, D_MODEL) @ w_o
    y_ctx = None
    if need_ctx_out:
        q_c = rms_norm(heads(h_ctx @ w_qkv[:, :D_MODEL]), q_g)
        s = jnp.einsum('bqhd,bkhd->bhqk', q_c, k_c).astype(jnp.float32) * scale
        p = jax.nn.softmax(s, axis=-1).astype(v_c.dtype)
        y_ctx = jnp.einsum('bhqk,bkhd->bqhd', p, v_c).reshape(B, n_ctx, D_MODEL) @ w_o
    return y_ctx, y_lat


def hgrn2_lower_bounds(lb_logits):
    p = jax.nn.softmax(lb_logits.astype(jnp.float32), axis=0)
    return jnp.cumsum(p, axis=0) - p[0:1]


def gla_chunk_scan(q, k, v, log_f, s0):
    B, L, H, _ = q.shape
    n = L // HG_CHUNK

    def chunks(t):
        return jnp.transpose(t.reshape(B, n, HG_CHUNK, H, t.shape[-1]), (1, 0, 3, 2, 4))

    causal = np.tril(np.ones((HG_CHUNK, HG_CHUNK), bool))[None, None, :, :, None]

    def step(S, inp):
        qc, kc, vc, gc = inp
        b = jnp.cumsum(gc, axis=2)
        o = jnp.einsum('bhtk,bhkv->bhtv', qc * jnp.exp(b), S)
        dec = jnp.exp(jnp.where(causal, b[:, :, :, None, :] - b[:, :, None, :, :], -jnp.inf))
        A = jnp.einsum('bhtk,bhsk,bhtsk->bhts', qc, kc, dec)
        o = o + jnp.einsum('bhts,bhsv->bhtv', A, vc)
        b_end = b[:, :, -1:, :]
        S = (jnp.exp(b_end[:, :, 0, :, None]) * S
             + jnp.einsum('bhsk,bhsv->bhkv', kc * jnp.exp(b_end - b), vc))
        return S, o

    S, o = lax.scan(step, s0, (chunks(q), chunks(k), chunks(v), chunks(log_f)))
    return jnp.transpose(o, (1, 0, 3, 2, 4)).reshape(B, L, H, v.shape[-1]), S


def hgrn2_mixer(h_ctx, h_lat, w_in, lb, norm_g, w_o, need_ctx_out):
    lbh = lb.reshape(HG_HEADS, HG_DK)

    def project(h):
        B, L, _ = h.shape

        def heads(t, dd):
            return t.reshape(B, L, HG_HEADS, dd).astype(jnp.float32)

        q, i, g, z_fwd, z_bwd = jnp.split(h @ w_in, 5, axis=-1)
        gates = []
        for z in (z_fwd, z_bwd):
            z = heads(z, HG_DK)
            k = (1 - lbh) * jax.nn.sigmoid(-z)
            log_f = jnp.log(lbh + (1 - lbh) * jax.nn.sigmoid(z))
            gates.append((k, log_f))
        return heads(jax.nn.silu(q), HG_DK), heads(i, HG_DV), g, gates

    q_c, v_c, g_c, gates_c = project(h_ctx)
    q_x, v_x, g_x, gates_x = project(h_lat)
    s0 = jnp.zeros((h_lat.shape[0], HG_HEADS, HG_DK, HG_DV), jnp.float32)
    o_c = jnp.zeros(v_c.shape, jnp.float32)
    o_x = jnp.zeros(v_x.shape, jnp.float32)
    for d in range(2):
        oc, s_ctx = gla_chunk_scan(flip_seq(q_c, d), flip_seq(gates_c[d][0], d), flip_seq(v_c, d),
                                   flip_seq(gates_c[d][1], d), s0)
        ox, _ = gla_chunk_scan(flip_seq(q_x, d), flip_seq(gates_x[d][0], d), flip_seq(v_x, d),
                               flip_seq(gates_x[d][1], d), s_ctx)
        o_x = o_x + flip_seq(ox, d)
        if need_ctx_out:
            o_c = o_c + flip_seq(oc, d)

    def readout(o, g):
        o = rms_norm(o, norm_g).reshape(g.shape).astype(g.dtype)
        return (o * jax.nn.silu(g)) @ w_o

    y_ctx = readout(o_c, g_c) if need_ctx_out else None
    return y_ctx, readout(o_x, g_x)


def setup_inputs(seed: int = 0) -> dict:
    key = jax.random.key(seed)
    ks = iter(jax.random.split(key, 32))

    def nrm(shape, scale):
        return scale * jax.random.normal(next(ks), shape, jnp.float32)

    n_a, n_b, n_c = [len(range(kind, DEPTH, N_MIXERS)) for kind in range(N_MIXERS)]
    u = jax.random.uniform(next(ks), (n_a, 2, D_RNN), jnp.float32, 0.9, 0.999)
    s = u ** (1.0 / LRU_C)
    inp = {}
    inp['x'] = nrm((BATCH, SEQ, D_MODEL), 1.0)
    inp['c'] = nrm((BATCH, D_MODEL), 1.0)
    inp['ctx'] = nrm((BATCH, CTX_LEN, D_MODEL), 1.0)
    inp['c_ctx'] = nrm((D_MODEL,), 1.0)
    inp['mod_w'] = nrm((DEPTH, D_MODEL, 6 * D_MODEL), D_MODEL ** -0.5)
    inp['mod_b'] = nrm((DEPTH, 6 * D_MODEL), 0.02)
    inp['norm_mix_g'] = 1.0 + nrm((DEPTH, D_MODEL), 0.02)
    inp['norm_ffn_g'] = 1.0 + nrm((DEPTH, D_MODEL), 0.02)
    inp['ffn_w_gu'] = nrm((DEPTH, D_MODEL, 2 * D_FF), D_MODEL ** -0.5)
    inp['ffn_w_down'] = nrm((DEPTH, D_FF, D_MODEL), D_FF ** -0.5)
    inp['lru_w_in'] = nrm((n_a, D_MODEL, 2 * D_RNN), D_MODEL ** -0.5)
    inp['lru_conv_w'] = nrm((n_a, CONV_W, D_RNN), CONV_W ** -0.5)
    inp['lru_conv_b'] = nrm((n_a, D_RNN), 0.02)
    inp['lru_gate_w'] = nrm((n_a, 2, 2, LRU_BLOCKS, LRU_BLOCK_W, LRU_BLOCK_W), LRU_BLOCK_W ** -0.5)
    inp['lru_gate_b'] = nrm((n_a, 2, 2, D_RNN), 0.02)
    inp['lru_lambda'] = jnp.log(s) - jnp.log1p(-s)
    inp['lru_w_out'] = nrm((n_a, D_RNN, D_MODEL), D_RNN ** -0.5)
    inp['na_w_qkv'] = nrm((n_b, D_MODEL, 3 * D_MODEL), D_MODEL ** -0.5)
    inp['na_q_norm_g'] = 1.0 + nrm((n_b, NA_HEAD_DIM), 0.02)
    inp['na_k_norm_g'] = 1.0 + nrm((n_b, NA_HEAD_DIM), 0.02)
    inp['na_rpb'] = nrm((n_b, NA_HEADS, 2 * WIN_ROWS - 1, 2 * WIN_COLS - 1), 0.05)
    inp['na_w_o'] = nrm((n_b, D_MODEL, D_MODEL), D_MODEL ** -0.5)
    inp['hg_w_in'] = nrm((n_c, D_MODEL, 5 * D_MODEL), D_MODEL ** -0.5)
    inp['hg_lb_logits'] = nrm((DEPTH, D_MODEL), 0.1)
    inp['hg_norm_g'] = 1.0 + nrm((n_c, HG_DV), 0.02)
    inp['hg_w_o'] = nrm((n_c, D_MODEL, D_MODEL), D_MODEL ** -0.5)
    return inp


def reference(x, c, ctx, c_ctx, mod_w, mod_b, norm_mix_g, norm_ffn_g, ffn_w_gu, ffn_w_down,
              lru_w_in, lru_conv_w, lru_conv_b, lru_gate_w, lru_gate_b, lru_lambda, lru_w_out,
              na_w_qkv, na_q_norm_g, na_k_norm_g, na_rpb, na_w_o,
              hg_w_in, hg_lb_logits, hg_norm_g, hg_w_o):
    lower_bounds = hgrn2_lower_bounds(hg_lb_logits)
    act_lat = jax.nn.silu(c)
    act_ctx = jax.nn.silu(c_ctx)
    for layer in range(DEPTH):
        kind, slot = layer % N_MIXERS, layer // N_MIXERS
        need_ctx = layer < DEPTH - 1
        mx = jnp.split((act_lat @ mod_w[layer] + mod_b[layer])[:, None, :], 6, axis=-1)
        mc = jnp.split(act_ctx @ mod_w[layer] + mod_b[layer], 6, axis=-1)
        hx = modulate(rms_norm(x, norm_mix_g[layer]), mx[0], mx[1])
        hc = modulate(rms_norm(ctx, norm_mix_g[layer]), mc[0], mc[1])
        if kind == 0:
            yc, yx = rglru_mixer(hc, hx, lru_w_in[slot], lru_conv_w[slot], lru_conv_b[slot],
                                 lru_gate_w[slot], lru_gate_b[slot], lru_lambda[slot], lru_w_out[slot],
                                 need_ctx)
        elif kind == 1:
            yc, yx = na_mixer(hc, hx, na_w_qkv[slot], na_q_norm_g[slot], na_k_norm_g[slot],
                              na_rpb[slot], na_w_o[slot], need_ctx)
        else:
            yc, yx = hgrn2_mixer(hc, hx, hg_w_in[slot], lower_bounds[layer], hg_norm_g[slot],
                                 hg_w_o[slot], need_ctx)
        x = x + mx[2] * yx
        x = x + mx[5] * swiglu(modulate(rms_norm(x, norm_ffn_g[layer]), mx[3], mx[4]),
                               ffn_w_gu[layer], ffn_w_down[layer])
        if need_ctx:
            ctx = ctx + mc[2] * yc
            ctx = ctx + mc[5] * swiglu(modulate(rms_norm(ctx, norm_ffn_g[layer]), mc[3], mc[4]),
                                       ffn_w_gu[layer], ffn_w_down[layer])
    return x
```

```python
import functools

import jax
import jax.numpy as jnp
from jax import lax
from jax.experimental import pallas as pl
from jax.experimental.pallas import tpu as pltpu

F32 = jnp.float32
BF16 = jnp.bfloat16

SUBLANES = 8
LANES = 128
EPS = 1e-6
LRU_C = 8.0
LRU_BLOCK_W = 128
CONV_W = 4
CONV_LEFT = 2
VMEM_LIMIT = 56 * 1024 * 1024


def _cparams(sem):
    return pltpu.CompilerParams(dimension_semantics=sem, vmem_limit_bytes=VMEM_LIMIT)


def _const_spec(shape):
    nd = len(shape)
    return pl.BlockSpec(shape, lambda *_: (0,) * nd, pipeline_mode=pl.Buffered(1))


def _sigmoid(x):
    return 0.5 * jnp.tanh(0.5 * x) + 0.5


def _silu(x):
    return x * _sigmoid(x)


def _gelu_tanh(x):
    return 0.5 * x * (1.0 + jnp.tanh(0.7978845608028654 * (x + 0.044715 * (x * x * x))))


def _t3(x):
    return x.reshape(x.shape[0] // SUBLANES, SUBLANES, x.shape[1])


def _norm_mod(x, g, shift, scale):
    ms = jnp.mean(x * x, axis=-1, keepdims=True)
    y = x * lax.rsqrt(ms + EPS) * g
    y3 = _t3(y) * (1.0 + scale)[None] + shift[None]
    return y3.reshape(x.shape)


def _mod_kernel(c_ref, w_ref, b_ref, o_ref):
    a = _silu(c_ref[...])
    o_ref[0] = jnp.dot(a, w_ref[0], preferred_element_type=F32) + b_ref[0]


def _modulation(cc, mod_w, mod_b):
    depth, d, d6 = mod_w.shape
    nblk = d6 // d
    return pl.pallas_call(
        _mod_kernel,
        grid=(depth, nblk),
        in_specs=[
            pl.BlockSpec((2 * SUBLANES, d), lambda l, j: (0, 0)),
            pl.BlockSpec((1, d, d), lambda l, j: (l, 0, j)),
            pl.BlockSpec((1, 1, d), lambda l, j: (l, 0, j)),
        ],
        out_specs=pl.BlockSpec((1, 2 * SUBLANES, d), lambda l, j: (l, 0, j)),
        out_shape=jax.ShapeDtypeStruct((depth, 2 * SUBLANES, d6), F32),
        compiler_params=_cparams(("arbitrary", "arbitrary")),
        name="modulation",
    )(cc, mod_w, mod_b.reshape(depth, 1, d6))


def _out_ffn_kernel(x_ref, z_ref, m_ref, gf_ref, wo_ref, wgu_ref, wd_ref, o_ref, *scratch):
    d = x_ref.shape[1]
    f = wd_ref.shape[0]
    x = x_ref[...]
    if scratch:
        (z_scr,) = scratch
        tl = z_ref.shape[1]
        for b in range(SUBLANES):
            for s in range(z_ref.shape[2] // LANES):
                z_scr[s, pl.ds(b, tl, stride=SUBLANES), :] = (
                    z_ref[b, :, s * LANES:(s + 1) * LANES].astype(F32))
        z = jnp.concatenate([z_scr[s] for s in range(z_scr.shape[0])], axis=1).astype(BF16)
    else:
        z = z_ref[...]
    y = jnp.dot(z, wo_ref[...], preferred_element_type=F32)
    x1 = (_t3(x) + m_ref[:, 2 * d:3 * d][None] * _t3(y)).reshape(x.shape)
    h = _norm_mod(x1, gf_ref[...], m_ref[:, 3 * d:4 * d], m_ref[:, 4 * d:5 * d])
    gu = jnp.dot(h.astype(BF16), wgu_ref[...], preferred_element_type=F32)
    act = _silu(gu[:, :f]) * gu[:, f:]
    y2 = jnp.dot(act.astype(BF16), wd_ref[...], preferred_element_type=F32)
    o_ref[...] = (_t3(x1) + m_ref[:, 5 * d:6 * d][None] * _t3(y2)).reshape(x.shape)


def _out_ffn(x, z, m, gf, wo, wgu, wd, tm):
    rows, d = x.shape
    kz = z.shape[-1]
    f = wd.shape[0]
    if z.ndim == 3:
        z_spec = pl.BlockSpec((SUBLANES, tm // SUBLANES, kz), lambda i: (0, i, 0))
        scratch = [pltpu.VMEM((kz // LANES, tm, LANES), F32)]
    else:
        z_spec = pl.BlockSpec((tm, kz), lambda i: (i, 0))
        scratch = []
    return pl.pallas_call(
        _out_ffn_kernel,
        grid=(rows // tm,),
        scratch_shapes=scratch,
        in_specs=[
            pl.BlockSpec((tm, d), lambda i: (i, 0)),
            z_spec,
            _const_spec((SUBLANES, 6 * d)),
            _const_spec((1, d)),
            _const_spec((kz, d)),
            _const_spec((d, 2 * f)),
            _const_spec((f, d)),
        ],
        out_specs=pl.BlockSpec((tm, d), lambda i: (i, 0)),
        out_shape=jax.ShapeDtypeStruct((rows, d), F32),
        compiler_params=_cparams(("arbitrary",)),
        name="out_ffn",
    )(x, z, m, gf, wo, wgu, wd)


def _inproj_lru_kernel(x_ref, m_ref, g_ref, w_ref, gate_ref, u_ref):
    d = x_ref.shape[1]
    c = u_ref.shape[1]
    h = _norm_mod(x_ref[...], g_ref[...], m_ref[:, 0:d], m_ref[:, d:2 * d])
    y = jnp.dot(h.astype(BF16), w_ref[...], preferred_element_type=F32)
    gate_ref[...] = _gelu_tanh(y[:, :c]).astype(BF16)
    u_ref[...] = y[:, c:]


def _inproj_lru(x, m, g, w_in, tm):
    rows, d = x.shape
    c = w_in.shape[1] // 2
    return pl.pallas_call(
        _inproj_lru_kernel,
        grid=(rows // tm,),
        in_specs=[
            pl.BlockSpec((tm, d), lambda i: (i, 0)),
            _const_spec((SUBLANES, 6 * d)),
            _const_spec((1, d)),
            _const_spec((d, 2 * c)),
        ],
        out_specs=[pl.BlockSpec((tm, c), lambda i: (i, 0)),
                   pl.BlockSpec((tm, c), lambda i: (i, 0))],
        out_shape=[jax.ShapeDtypeStruct((rows, c), BF16),
                   jax.ShapeDtypeStruct((rows, c), F32)],
        compiler_params=_cparams(("arbitrary",)),
        name="inproj_lru",
    )(x, m, g, w_in)


def _lru_scan_kernel(*refs, nblk, reverse, combine):
    if combine:
        (u_ref, up_ref, un_ref, cw_ref, cb_ref, gw_ref, gb_ref, lam_ref, h0_ref,
         hother_ref, gate_ref, out_ref, hlast_ref, ext_scr, a_scr, b_scr, h_scr) = refs
    else:
        (u_ref, up_ref, un_ref, cw_ref, cb_ref, gw_ref, gb_ref, lam_ref, h0_ref,
         out_ref, hlast_ref, ext_scr, a_scr, b_scr, h_scr) = refs
    i = pl.program_id(0)
    j = (nblk - 1 - i) if reverse else i
    rows, c = u_ref.shape
    steps = rows // SUBLANES
    halo_l = CONV_LEFT * SUBLANES
    halo_r = (CONV_W - 1 - CONV_LEFT) * SUBLANES
    nblocks = c // LRU_BLOCK_W

    @pl.when(i == 0)
    def _():
        h_scr[...] = h0_ref[...]

    ext_scr[0:halo_l, :] = jnp.where(j > 0, up_ref[...], 0.0)
    ext_scr[halo_l:halo_l + rows, :] = u_ref[...]
    ext_scr[halo_l + rows:, :] = jnp.where(j < nblk - 1, un_ref[...], 0.0)

    for n in range(nblocks):
        ls = slice(n * LRU_BLOCK_W, (n + 1) * LRU_BLOCK_W)
        u = cb_ref[:, ls] + cw_ref[0:1, ls] * ext_scr[0:rows, ls]
        for k in range(1, CONV_W):
            u = u + cw_ref[k:k + 1, ls] * ext_scr[k * SUBLANES:k * SUBLANES + rows, ls]
        z = jnp.dot(u.astype(BF16), gw_ref[n], preferred_element_type=F32) + gb_ref[n]
        r = _sigmoid(z[:, :LRU_BLOCK_W])
        ig = _sigmoid(z[:, LRU_BLOCK_W:])
        lam = lam_ref[:, ls]
        softplus_neg = jnp.maximum(-lam, 0.0) + jnp.log1p(jnp.exp(-jnp.abs(lam)))
        log_a = (-LRU_C * softplus_neg) * r
        a = jnp.exp(log_a)
        one_m_a2 = -jnp.tanh(log_a) * (1.0 + a * a)
        a_scr[:, ls] = a
        b_scr[:, ls] = jnp.sqrt(one_m_a2) * ig * u

    def step(s, h):
        t = (steps - 1 - s) if reverse else s
        off = pl.multiple_of(t * SUBLANES, SUBLANES)
        h = a_scr[pl.ds(off, SUBLANES), :] * h + b_scr[pl.ds(off, SUBLANES), :]
        b_scr[pl.ds(off, SUBLANES), :] = h
        return h

    h_fin = lax.fori_loop(0, steps, step, h_scr[...])
    h_scr[...] = h_fin
    hlast_ref[...] = h_fin
    if combine:
        out_ref[...] = ((hother_ref[...] + b_scr[...]) * gate_ref[...].astype(F32)).astype(out_ref.dtype)
    else:
        out_ref[...] = b_scr[...]


def _lru_scan(u, conv_w, conv_b, gw, gb, lam, h0, *, reverse, tl, h_other=None, gate=None):
    rows_total, c = u.shape
    rows = tl * SUBLANES
    nblk = rows_total // rows
    halo_l = CONV_LEFT * SUBLANES
    halo_r = (CONV_W - 1 - CONV_LEFT) * SUBLANES
    combine = h_other is not None
    nblocks = c // LRU_BLOCK_W

    def blk(i):
        return (nblk - 1 - i) if reverse else i

    in_specs = [
        pl.BlockSpec((rows, c), lambda i: (blk(i), 0)),
        pl.BlockSpec((halo_l, c), lambda i: (jnp.maximum(blk(i) * (rows // halo_l) - 1, 0), 0)),
        pl.BlockSpec((halo_r, c), lambda i: (jnp.minimum((blk(i) + 1) * (rows // halo_r),
                                                         rows_total // halo_r - 1), 0)),
        _const_spec((CONV_W, c)),
        _const_spec((1, c)),
        _const_spec((nblocks, LRU_BLOCK_W, 2 * LRU_BLOCK_W)),
        _const_spec((nblocks, 1, 2 * LRU_BLOCK_W)),
        _const_spec((1, c)),
        _const_spec((SUBLANES, c)),
    ]
    args = [u, u, u, conv_w, conv_b, gw, gb, lam, h0]
    if combine:
        in_specs += [pl.BlockSpec((rows, c), lambda i: (blk(i), 0)),
                     pl.BlockSpec((rows, c), lambda i: (blk(i), 0))]
        args += [h_other, gate]
    out_dtype = BF16 if combine else F32
    return pl.pallas_call(
        functools.partial(_lru_scan_kernel, nblk=nblk, reverse=reverse, combine=combine),
        grid=(nblk,),
        in_specs=in_specs,
        out_specs=[pl.BlockSpec((rows, c), lambda i: (blk(i), 0)),
                   pl.BlockSpec((SUBLANES, c), lambda i: (0, 0))],
        out_shape=[jax.ShapeDtypeStruct((rows_total, c), out_dtype),
                   jax.ShapeDtypeStruct((SUBLANES, c), F32)],
        scratch_shapes=[
            pltpu.VMEM((halo_l + rows + halo_r, c), F32),
            pltpu.VMEM((rows, c), F32),
            pltpu.VMEM((rows, c), F32),
            pltpu.VMEM((SUBLANES, c), F32),
        ],
        compiler_params=_cparams(("arbitrary",)),
        name="lru_scan_bwd" if reverse else "lru_scan_fwd",
    )(*args)


def _lru_mixer(xc, xl, mc, ml, g, p, need_ctx, tm, tl):
    w_in, conv_w, conv_b, gw, gb, lam = p
    c = w_in.shape[1] // 2
    gate_c, u_c = _inproj_lru(xc, mc, g, w_in, tm)
    gate_l, u_l = _inproj_lru(xl, ml, g, w_in, tm)
    h0 = jnp.zeros((SUBLANES, c), F32)
    tl_c = min(tl, xc.shape[0] // SUBLANES)
    hf_c, s_c = _lru_scan(u_c, conv_w, conv_b, gw[0], gb[0], lam[0], h0, reverse=False, tl=tl_c)
    hf_l, _ = _lru_scan(u_l, conv_w, conv_b, gw[0], gb[0], lam[0], s_c, reverse=False, tl=tl)
    if need_ctx:
        z_c, s_c = _lru_scan(u_c, conv_w, conv_b, gw[1], gb[1], lam[1], h0, reverse=True, tl=tl_c,
                             h_other=hf_c, gate=gate_c)
    else:
        z_c = None
        _, s_c = _lru_scan(u_c, conv_w, conv_b, gw[1], gb[1], lam[1], h0, reverse=True, tl=tl_c)
    z_l, _ = _lru_scan(u_l, conv_w, conv_b, gw[1], gb[1], lam[1], s_c, reverse=True, tl=tl,
                       h_other=hf_l, gate=gate_l)
    return z_c, z_l


def _prep_lru(w_in, conv_w, conv_b, gate_w, gate_b, lam):
    nb = gate_w.shape[2]
    gw = jnp.concatenate([gate_w[:, 0], gate_w[:, 1]], axis=-1).astype(BF16)
    gb = jnp.concatenate([gate_b[:, 0].reshape(2, nb, 1, LRU_BLOCK_W),
                          gate_b[:, 1].reshape(2, nb, 1, LRU_BLOCK_W)], axis=-1)
    return (w_in.astype(BF16), conv_w, conv_b.reshape(1, -1), gw, gb, lam.reshape(2, 1, -1))


NA_HEAD_DIM = 64
GRID_W = 64
WIN_ROWS = 8
WIN_COLS = 16
NEG = -1e30


def _inproj_na_kernel(x_ref, m_ref, g_ref, w_ref, qg_ref, kg_ref, ones_ref, q_ref, k_ref, v_ref, y_scr):
    d = x_ref.shape[1]
    tl = x_ref.shape[0] // SUBLANES
    h = _norm_mod(x_ref[...], g_ref[...], m_ref[:, 0:d], m_ref[:, d:2 * d])
    y = jnp.dot(h.astype(BF16), w_ref[...], preferred_element_type=F32)
    cw = ones_ref.shape[0]
    for part, gain_ref in ((0, qg_ref), (1, kg_ref)):
        for j in range(d // cw):
            lo = part * d + j * cw
            t = y[:, lo:lo + cw]
            ss = jnp.dot((t * t).astype(BF16), ones_ref[...], preferred_element_type=F32)
            tn = t * lax.rsqrt(ss * (1.0 / NA_HEAD_DIM) + EPS) * gain_ref[:, j * cw:(j + 1) * cw]
            for s in range(cw // LANES):
                y_scr[lo // LANES + s] = tn[:, s * LANES:(s + 1) * LANES]
    for s in range(d // LANES):
        y_scr[2 * d // LANES + s] = y[:, 2 * d + s * LANES:2 * d + (s + 1) * LANES]
    for b in range(SUBLANES):
        for part, o_ref in ((0, q_ref), (1, k_ref), (2, v_ref)):
            for s in range(d // LANES):
                o_ref[b, :, s * LANES:(s + 1) * LANES] = (
                    y_scr[part * (d // LANES) + s, pl.ds(b, tl, stride=SUBLANES), :].astype(BF16))


def _inproj_na(x, m, g, w_qkv, qg, kg, ones_bd, tm):
    rows, d = x.shape
    t_total = rows // SUBLANES
    tl = tm // SUBLANES
    out = jax.ShapeDtypeStruct((SUBLANES, t_total, d), BF16)
    ospec = pl.BlockSpec((SUBLANES, tl, d), lambda i: (0, i, 0))
    return pl.pallas_call(
        _inproj_na_kernel,
        grid=(rows // tm,),
        in_specs=[
            pl.BlockSpec((tm, d), lambda i: (i, 0)),
            _const_spec((SUBLANES, 6 * d)),
            _const_spec((1, d)),
            _const_spec((d, 3 * d)),
            _const_spec((1, d)),
            _const_spec((1, d)),
            _const_spec(ones_bd.shape),
        ],
        out_specs=[ospec, ospec, ospec],
        out_shape=[out, out, out],
        scratch_shapes=[pltpu.VMEM((3 * d // LANES, tm, LANES), F32)],
        compiler_params=_cparams(("arbitrary",)),
        name="inproj_na",
    )(x, m, g, w_qkv, qg, kg, ones_bd)


def _rpb_expand_kernel(rpb_ref, o_ref):
    h = pl.program_id(0)
    n_ro, n_co = rpb_ref.shape[1], rpb_ref.shape[2]
    c = lax.broadcasted_iota(jnp.int32, (GRID_W, GRID_W), 0)
    q = lax.broadcasted_iota(jnp.int32, (GRID_W, GRID_W), 1)
    cs = jnp.clip(q - WIN_COLS // 2, 0, GRID_W - WIN_COLS)
    valid = (c >= cs) & (c < cs + WIN_COLS)
    off = c - q + (WIN_COLS - 1)
    for ro in range(n_ro):
        acc = jnp.full((GRID_W, GRID_W), NEG, F32)
        for j in range(n_co):
            acc = jnp.where(off == j, rpb_ref[h, ro, j], acc)
        o_ref[0, ro] = jnp.where(valid, acc, NEG)


def _rpb_tables(rpb):
    nh, n_ro, _ = rpb.shape
    m = pl.pallas_call(
        _rpb_expand_kernel,
        grid=(nh,),
        in_specs=[pl.BlockSpec(memory_space=pltpu.SMEM)],
        out_specs=pl.BlockSpec((1, n_ro, GRID_W, GRID_W), lambda h: (h, 0, 0, 0)),
        out_shape=jax.ShapeDtypeStruct((nh, n_ro, GRID_W, GRID_W), F32),
        compiler_params=_cparams(("arbitrary",)),
        name="rpb_expand",
    )(rpb)
    win = jnp.stack([m[:, r0:r0 + WIN_ROWS] for r0 in range(n_ro - WIN_ROWS + 1)])
    win = win.reshape(win.shape[0], nh // 2, 2, WIN_ROWS, GRID_W, GRID_W)
    win = win.transpose(0, 1, 3, 4, 2, 5)
    return win.reshape(win.shape[0], nh // 2, WIN_ROWS * GRID_W, 2 * GRID_W)


def _attn_pair(qp, parts):
    nq = qp.shape[0]
    lane = lax.broadcasted_iota(jnp.int32, qp.shape, 1)
    zero = jnp.zeros_like(qp)
    qbd = jnp.concatenate([jnp.where(lane < NA_HEAD_DIM, qp, zero),
                           jnp.where(lane >= NA_HEAD_DIM, qp, zero)], axis=0)
    nt = (((1,), (1,)), ((), ()))
    scores = []
    for k, _, bias in parts:
        s = lax.dot_general(k, qbd, nt, preferred_element_type=F32)
        scores.append(s if bias is None else s + bias)
    mx = functools.reduce(jnp.maximum, [jnp.max(s, axis=0, keepdims=True) for s in scores])
    es = [jnp.exp(s - mx) for s in scores]
    inv = 1.0 / functools.reduce(jnp.add, [jnp.sum(e, axis=0, keepdims=True) for e in es])
    tn = (((0,), (0,)), ((), ()))
    r = functools.reduce(jnp.add, [
        lax.dot_general((e * inv).astype(BF16), v, tn, preferred_element_type=F32)
        for e, (_, v, _) in zip(es, parts)])
    lane_o = lax.broadcasted_iota(jnp.int32, (nq, LANES), 1)
    return jnp.where(lane_o < NA_HEAD_DIM, r[:nq], r[nq:])


def _na_attn_kernel(q_ref, k_ref, v_ref, kc_ref, vc_ref, bias_ref, o_ref, *, rows):
    r = pl.program_id(1)
    rs = jnp.clip(r - WIN_ROWS // 2, 0, rows - WIN_ROWS)
    start = pl.multiple_of(rs * GRID_W, GRID_W)
    nwin = WIN_ROWS * GRID_W
    for p in range(q_ref.shape[2] // LANES):
        ls = slice(p * LANES, (p + 1) * LANES)
        o = _attn_pair(q_ref[0, :, ls], [
            (k_ref[0, pl.ds(start, nwin), ls], v_ref[0, pl.ds(start, nwin), ls], bias_ref[0, p]),
            (kc_ref[0, :, ls], vc_ref[0, :, ls], None)])
        o_ref[0, :, ls] = o.astype(o_ref.dtype)


def _na_attn(q, k, v, kc, vc, bias_t):
    b, l, d = q.shape
    ct = kc.shape[1]
    rows = l // GRID_W
    half = WIN_ROWS // 2

    def bias_idx(bi, r):
        rs = jnp.clip(r - half, 0, rows - WIN_ROWS)
        return (rs - r + (WIN_ROWS - 1), 0, 0, 0)

    return pl.pallas_call(
        functools.partial(_na_attn_kernel, rows=rows),
        grid=(b, rows),
        in_specs=[
            pl.BlockSpec((1, GRID_W, d), lambda bi, r: (bi, r, 0)),
            pl.BlockSpec((1, l, d), lambda bi, r: (bi, 0, 0), pipeline_mode=pl.Buffered(1)),
            pl.BlockSpec((1, l, d), lambda bi, r: (bi, 0, 0), pipeline_mode=pl.Buffered(1)),
            pl.BlockSpec((1, ct, d), lambda bi, r: (bi, 0, 0)),
            pl.BlockSpec((1, ct, d), lambda bi, r: (bi, 0, 0)),
            pl.BlockSpec((1,) + bias_t.shape[1:], bias_idx),
        ],
        out_specs=pl.BlockSpec((1, GRID_W, d), lambda bi, r: (bi, r, 0)),
        out_shape=jax.ShapeDtypeStruct((b, l, d), BF16),
        compiler_params=_cparams(("arbitrary", "arbitrary")),
        name="na_attn",
    )(q, k, v, kc, vc, bias_t)


def _ctx_attn_kernel(q_ref, k_ref, v_ref, o_ref):
    for p in range(q_ref.shape[2] // LANES):
        ls = slice(p * LANES, (p + 1) * LANES)
        o = _attn_pair(q_ref[0, :, ls], [(k_ref[0, :, ls], v_ref[0, :, ls], None)])
        o_ref[0, :, ls] = o.astype(o_ref.dtype)


def _ctx_attn(q, k, v):
    b, ct, d = q.shape
    spec = pl.BlockSpec((1, ct, d), lambda bi: (bi, 0, 0))
    return pl.pallas_call(
        _ctx_attn_kernel,
        grid=(b,),
        in_specs=[spec, spec, spec],
        out_specs=spec,
        out_shape=jax.ShapeDtypeStruct((b, ct, d), BF16),
        compiler_params=_cparams(("arbitrary",)),
        name="ctx_attn",
    )(q, k, v)


def _prep_na(w_qkv, qg, kg):
    d = w_qkv.shape[0]
    nh = d // NA_HEAD_DIM
    head = jnp.arange(2 * LANES) // NA_HEAD_DIM
    ones_bd = (head[:, None] == head[None, :]).astype(BF16)
    qg_t = jnp.tile(qg, nh).reshape(1, d) * (NA_HEAD_DIM ** -0.5)
    kg_t = jnp.tile(kg, nh).reshape(1, d)
    return w_qkv.astype(BF16), qg_t, kg_t, ones_bd


def _na_mixer(xc, xl, mc, ml, g, p, bias_t, need_ctx, tm):
    w_qkv, qg_t, kg_t, ones_bd = p
    q_c, k_c, v_c = _inproj_na(xc, mc, g, w_qkv, qg_t, kg_t, ones_bd, tm)
    q_l, k_l, v_l = _inproj_na(xl, ml, g, w_qkv, qg_t, kg_t, ones_bd, tm)
    o_l = _na_attn(q_l, k_l, v_l, k_c, v_c, bias_t)
    o_c = _ctx_attn(q_c, k_c, v_c) if need_ctx else None
    return o_c, o_l


HG_DK = 128
HG_CHUNK = 64
HG_SUB = 16


def _lb_kernel(x_ref, o_ref):
    x = x_ref[...]
    e = jnp.exp(x - jnp.max(x, axis=0, keepdims=True))
    p = e / jnp.sum(e, axis=0, keepdims=True)
    acc = jnp.zeros_like(p[0:1])
    for l in range(x.shape[0]):
        o_ref[l:l + 1, :] = acc
        if l + 1 < x.shape[0]:
            acc = acc + p[l + 1:l + 2]


def _lower_bounds(lb_logits):
    return pl.pallas_call(
        _lb_kernel,
        out_shape=jax.ShapeDtypeStruct(lb_logits.shape, F32),
        name="hg_lower_bounds",
    )(lb_logits)


def _inproj_hg_kernel(x_ref, m_ref, g_ref, w_ref, q_ref, v_ref, gs_ref, zf_ref, zb_ref):
    d = x_ref.shape[1]
    h = _norm_mod(x_ref[...], g_ref[...], m_ref[:, 0:d], m_ref[:, d:2 * d])
    y = jnp.dot(h.astype(BF16), w_ref[...], preferred_element_type=F32)
    q_ref[...] = _silu(y[:, 0:d]).astype(BF16)
    v_ref[...] = y[:, d:2 * d].astype(BF16)
    gs_ref[...] = _silu(y[:, 2 * d:3 * d]).astype(BF16)
    zf_ref[...] = y[:, 3 * d:4 * d]
    zb_ref[...] = y[:, 4 * d:5 * d]


def _inproj_hg(x, m, g, w_in, tm):
    rows, d = x.shape
    spec = pl.BlockSpec((tm, d), lambda i: (i, 0))
    return pl.pallas_call(
        _inproj_hg_kernel,
        grid=(rows // tm,),
        in_specs=[spec, _const_spec((SUBLANES, 6 * d)), _const_spec((1, d)), _const_spec((d, 5 * d))],
        out_specs=[spec] * 5,
        out_shape=[jax.ShapeDtypeStruct((rows, d), BF16)] * 3 + [jax.ShapeDtypeStruct((rows, d), F32)] * 2,
        compiler_params=_cparams(("arbitrary",)),
        name="inproj_hg",
    )(x, m, g, w_in)


def _hg_scan_kernel(*refs, reverse, combine):
    if combine:
        (q_ref, v_ref, z_ref, lb_ref, s0_ref, oo_ref, gs_ref, ng_ref,
         out_ref, sout_ref, s_scr, qe_scr, ke_scr, qt_scr, kt_scr, v_scr, o_scr) = refs
    else:
        (q_ref, v_ref, z_ref, lb_ref, s0_ref,
         out_ref, sout_ref, s_scr, qe_scr, ke_scr, qt_scr, kt_scr, v_scr, o_scr) = refs
    h = pl.program_id(0)
    i = pl.program_id(1)
    nsteps = HG_CHUNK
    nsub = nsteps // HG_SUB
    rows = nsteps * SUBLANES

    @pl.when(i == 0)
    def _():
        s_scr[...] = s0_ref[0]

    @pl.when((i == 0) & (h == 0))
    def _():
        qe_scr[...] = jnp.zeros_like(qe_scr)
        ke_scr[...] = jnp.zeros_like(ke_scr)

    order = range(nsteps - 1, -1, -1) if reverse else range(nsteps)

    def load_steps(ref):
        return jnp.stack([ref[t * SUBLANES:(t + 1) * SUBLANES, :].astype(F32) for t in order])

    def store_steps(val3):
        return jnp.concatenate([val3[s] for s in order], axis=0)

    lb = lb_ref[...]
    q = load_steps(q_ref)
    v = load_steps(v_ref)
    sg = _sigmoid(load_steps(z_ref))
    k = (1.0 - lb) * (1.0 - sg)
    logf = jnp.log(lb + (1.0 - lb) * sg)
    acc = logf[0]
    cum = [acc]
    for s in range(1, nsteps):
        acc = acc + logf[s]
        cum.append(acc)
    b = jnp.stack(cum)
    b_end = cum[-1]

    v_scr[...] = v.reshape(rows, LANES)
    qt_scr[...] = (q * jnp.exp(b)).reshape(rows, LANES)
    kt_scr[...] = (k * jnp.exp(b_end[None] - b)).reshape(rows, LANES)
    for blk in range(1, nsub):
        lo = blk * HG_SUB
        beta = cum[lo - 1][None]
        qe_scr[blk - 1, lo * SUBLANES:(lo + HG_SUB) * SUBLANES, :] = (
            q[lo:lo + HG_SUB] * jnp.exp(b[lo:lo + HG_SUB] - beta)).reshape(HG_SUB * SUBLANES, LANES)
        ke_scr[blk - 1, 0:lo * SUBLANES, :] = (
            k[:lo] * jnp.exp(beta - b[:lo])).reshape(lo * SUBLANES, LANES)

    shp = (nsub, HG_SUB, SUBLANES, LANES)
    b4, q4, k4, v4 = b.reshape(shp), q.reshape(shp), k.reshape(shp), v.reshape(shp)
    o_near = jnp.sum(q4 * k4, axis=-1, keepdims=True) * v4
    for dlt in range(1, HG_SUB):
        n = HG_SUB - dlt
        w = q4[:, dlt:] * k4[:, :n] * jnp.exp(b4[:, dlt:] - b4[:, :n])
        contrib = jnp.sum(w, axis=-1, keepdims=True) * v4[:, :n]
        o_near = o_near + jnp.concatenate(
            [jnp.zeros((nsub, dlt, SUBLANES, LANES), F32), contrib], axis=1)

    nt = (((1,), (1,)), ((), ()))
    tn = (((0,), (0,)), ((), ()))
    decay_end = jnp.exp(b_end)
    for bb in range(SUBLANES):
        sl = pl.ds(bb, nsteps, stride=SUBLANES)
        qe = jnp.concatenate([qe_scr[j, sl, :] for j in range(nsub - 1)], axis=1).astype(BF16)
        ke = jnp.concatenate([ke_scr[j, sl, :] for j in range(nsub - 1)], axis=1).astype(BF16)
        a_far = lax.dot_general(qe, ke, nt, preferred_element_type=F32)
        vb = v_scr[sl, :].astype(BF16)
        st = s_scr[bb]
        o_b = (jnp.dot(a_far.astype(BF16), vb, preferred_element_type=F32)
               + lax.dot_general(qt_scr[sl, :].astype(BF16), st.astype(BF16), nt,
                                 preferred_element_type=F32))
        o_scr[sl, :] = o_b
        s_new = st * decay_end[bb:bb + 1, :] + lax.dot_general(
            vb, kt_scr[sl, :].astype(BF16), tn, preferred_element_type=F32)
        s_scr[bb] = s_new
        sout_ref[0, bb] = s_new

    o_tot = store_steps(o_scr[...].reshape(nsteps, SUBLANES, LANES) + o_near.reshape(nsteps, SUBLANES, LANES))
    if combine:
        tot = oo_ref[...] + o_tot
        y = tot * lax.rsqrt(jnp.mean(tot * tot, axis=-1, keepdims=True) + EPS) * ng_ref[...]
        out_ref[...] = (y * gs_ref[...].astype(F32)).astype(out_ref.dtype)
    else:
        out_ref[...] = o_tot


def _hg_scan(q, v, z, lb, s0, *, reverse, o_other=None, gs=None, norm_g=None):
    rows_total, d = q.shape
    nh = d // HG_DK
    rows = HG_CHUNK * SUBLANES
    nblk = rows_total // rows
    combine = o_other is not None

    def blk(i):
        return (nblk - 1 - i) if reverse else i

    tile = pl.BlockSpec((rows, HG_DK), lambda h, i: (blk(i), h))
    state = pl.BlockSpec((1, SUBLANES, HG_DK, HG_DK), lambda h, i: (h, 0, 0, 0))
    in_specs = [tile, tile, tile, pl.BlockSpec((1, HG_DK), lambda h, i: (0, h)), state]
    args = [q, v, z, lb, s0]
    if combine:
        in_specs += [tile, tile, pl.BlockSpec((1, HG_DK), lambda h, i: (0, 0))]
        args += [o_other, gs, norm_g]
    return pl.pallas_call(
        functools.partial(_hg_scan_kernel, reverse=reverse, combine=combine),
        grid=(nh, nblk),
        in_specs=in_specs,
        out_specs=[tile, state],
        out_shape=[jax.ShapeDtypeStruct((rows_total, d), BF16 if combine else F32),
                   jax.ShapeDtypeStruct(s0.shape, F32)],
        scratch_shapes=[
            pltpu.VMEM((SUBLANES, HG_DK, HG_DK), F32),
            pltpu.VMEM((HG_CHUNK // HG_SUB - 1, rows, LANES), F32),
            pltpu.VMEM((HG_CHUNK // HG_SUB - 1, rows, LANES), F32),
            pltpu.VMEM((rows, LANES), F32),
            pltpu.VMEM((rows, LANES), F32),
            pltpu.VMEM((rows, LANES), F32),
            pltpu.VMEM((rows, LANES), F32),
        ],
        compiler_params=_cparams(("arbitrary", "arbitrary")),
        name="hg_scan_bwd" if reverse else "hg_scan_fwd",
    )(*args)


def _hg_mixer(xc, xl, mc, ml, g, w_in, lb, norm_g, need_ctx, tm):
    d = xl.shape[1]
    nh = d // HG_DK
    q_c, v_c, gs_c, zf_c, zb_c = _inproj_hg(xc, mc, g, w_in, tm)
    q_l, v_l, gs_l, zf_l, zb_l = _inproj_hg(xl, ml, g, w_in, tm)
    s0 = jnp.zeros((nh, SUBLANES, HG_DK, HG_DK), F32)
    of_c, s_c = _hg_scan(q_c, v_c, zf_c, lb, s0, reverse=False)
    of_l, _ = _hg_scan(q_l, v_l, zf_l, lb, s_c, reverse=False)
    if need_ctx:
        z_c, s_c = _hg_scan(q_c, v_c, zb_c, lb, s0, reverse=True, o_other=of_c, gs=gs_c, norm_g=norm_g)
    else:
        z_c = None
        _, s_c = _hg_scan(q_c, v_c, zb_c, lb, s0, reverse=True)
    z_l, _ = _hg_scan(q_l, v_l, zb_l, lb, s_c, reverse=True, o_other=of_l, gs=gs_l, norm_g=norm_g)
    return z_c, z_l


def kernel(x, c, ctx, c_ctx, mod_w, mod_b, norm_mix_g, norm_ffn_g, ffn_w_gu, ffn_w_down, lru_w_in, lru_conv_w, lru_conv_b, lru_gate_w, lru_gate_b, lru_lambda, lru_w_out, na_w_qkv, na_q_norm_g, na_k_norm_g, na_rpb, na_w_o, hg_w_in, hg_lb_logits, hg_norm_g, hg_w_o):
    b, l, d = x.shape
    ct = ctx.shape[1]
    depth = mod_w.shape[0]
    tm = 256
    cc = jnp.zeros((2 * SUBLANES, d), F32).at[:b].set(c).at[b].set(c_ctx)
    mod = _modulation(cc, mod_w, mod_b)
    lower_bounds = _lower_bounds(hg_lb_logits)
    xl = x.transpose(1, 0, 2).reshape(l * b, d)
    xc = ctx.transpose(1, 0, 2).reshape(ct * b, d)
    for layer in range(depth):
        kind, slot = layer % 3, layer // 3
        need_ctx = layer < depth - 1
        ml = mod[layer, :b]
        mc = jnp.broadcast_to(mod[layer, b:b + 1], (b, 6 * d))
        g = norm_mix_g[layer].reshape(1, d)
        gf = norm_ffn_g[layer].reshape(1, d)
        wgu = ffn_w_gu[layer].astype(BF16)
        wd = ffn_w_down[layer].astype(BF16)
        if kind == 0:
            p = _prep_lru(lru_w_in[slot], lru_conv_w[slot], lru_conv_b[slot], lru_gate_w[slot],
                          lru_gate_b[slot], lru_lambda[slot])
            zc, zl = _lru_mixer(xc, xl, mc, ml, g, p, need_ctx, tm, 64)
            wo = lru_w_out[slot].astype(BF16)
        elif kind == 1:
            p = _prep_na(na_w_qkv[slot], na_q_norm_g[slot], na_k_norm_g[slot])
            zc, zl = _na_mixer(xc, xl, mc, ml, g, p, _rpb_tables(na_rpb[slot]), need_ctx, tm)
            wo = na_w_o[slot].astype(BF16)
        else:
            zc, zl = _hg_mixer(xc, xl, mc, ml, g, hg_w_in[slot].astype(BF16),
                               lower_bounds[layer:layer + 1], hg_norm_g[slot].reshape(1, -1), need_ctx, tm)
            wo = hg_w_o[slot].astype(BF16)
        xl = _out_ffn(xl, zl, ml, gf, wo, wgu, wd, tm)
        if need_ctx:
            xc = _out_ffn(xc, zc, mc, gf, wo, wgu, wd, tm)
    return xl.reshape(l, b, d).transpose(1, 0, 2)
```

```python
import functools

import jax
import jax.numpy as jnp
from jax import lax
from jax.experimental import pallas as pl
from jax.experimental.pallas import tpu as pltpu

F32 = jnp.float32
BF16 = jnp.bfloat16

SUBLANES = 8
LANES = 128
EPS = 1e-6
LRU_C = 8.0
LRU_BLOCK_W = 128
CONV_W = 4
CONV_LEFT = 2
VMEM_LIMIT = 56 * 1024 * 1024
MXU_TILE = 256
FFN_CHUNKS = 2
ROW_TILE = 512
LRU_STEPS = 64


def _split_aligned(n, parts, align):
    if n % align:
        return [(0, n)]
    tiles = n // align
    cuts = [round(i * tiles / parts) * align for i in range(parts + 1)]
    return [(lo, hi) for lo, hi in zip(cuts[:-1], cuts[1:]) if hi > lo]


def _cparams(sem):
    return pltpu.CompilerParams(dimension_semantics=sem, vmem_limit_bytes=VMEM_LIMIT)


def _const_spec(shape):
    nd = len(shape)
    return pl.BlockSpec(shape, lambda *_: (0,) * nd, pipeline_mode=pl.Buffered(1))


def _sigmoid(x):
    return 0.5 * jnp.tanh(0.5 * x) + 0.5


def _silu(x):
    return x * _sigmoid(x)


def _gelu_tanh(x):
    return 0.5 * x * (1.0 + jnp.tanh(0.7978845608028654 * (x + 0.044715 * (x * x * x))))


def _t3(x):
    return x.reshape(x.shape[0] // SUBLANES, SUBLANES, x.shape[1])


def _norm_mod(x, g, shift, scale):
    ms = jnp.mean(x * x, axis=-1, keepdims=True)
    y = x * lax.rsqrt(ms + EPS) * g
    y3 = _t3(y) * (1.0 + scale)[None] + shift[None]
    return y3.reshape(x.shape)


def _mod_kernel(c_ref, w_ref, b_ref, o_ref):
    a = _silu(c_ref[...])
    o_ref[0] = jnp.dot(a, w_ref[0], preferred_element_type=F32) + b_ref[0]


def _modulation(cc, mod_w, mod_b):
    depth, d, d6 = mod_w.shape
    nblk = d6 // d
    return pl.pallas_call(
        _mod_kernel,
        grid=(depth, nblk),
        in_specs=[
            pl.BlockSpec((2 * SUBLANES, d), lambda l, j: (0, 0)),
            pl.BlockSpec((1, d, d), lambda l, j: (l, 0, j)),
            pl.BlockSpec((1, 1, d), lambda l, j: (l, 0, j)),
        ],
        out_specs=pl.BlockSpec((1, 2 * SUBLANES, d), lambda l, j: (l, 0, j)),
        out_shape=jax.ShapeDtypeStruct((depth, 2 * SUBLANES, d6), F32),
        compiler_params=_cparams(("arbitrary", "arbitrary")),
        name="modulation",
    )(cc, mod_w, mod_b.reshape(depth, 1, d6))


def _out_ffn_kernel(x_ref, z_ref, m_ref, gf_ref, wo_ref, wgu_ref, wd_ref, o_ref, *scratch):
    d = x_ref.shape[1]
    f = wd_ref.shape[0]
    x = x_ref[...]
    if scratch:
        (z_scr,) = scratch
        tl = z_ref.shape[1]
        for b in range(SUBLANES):
            for s in range(z_ref.shape[2] // LANES):
                z_scr[s, pl.ds(b, tl, stride=SUBLANES), :] = (
                    z_ref[b, :, s * LANES:(s + 1) * LANES].astype(F32))
        z = jnp.concatenate([z_scr[s] for s in range(z_scr.shape[0])], axis=1).astype(BF16)
    else:
        z = z_ref[...]
    y = jnp.dot(z, wo_ref[...], preferred_element_type=F32)
    x1 = (_t3(x) + m_ref[:, 2 * d:3 * d][None] * _t3(y)).reshape(x.shape)
    h = _norm_mod(x1, gf_ref[...], m_ref[:, 3 * d:4 * d], m_ref[:, 4 * d:5 * d])
    hb = h.astype(BF16)
    y2 = None
    for lo, hi in _split_aligned(f, FFN_CHUNKS, MXU_TILE):
        a = jnp.dot(hb, wgu_ref[:, lo:hi], preferred_element_type=F32)
        g = jnp.dot(hb, wgu_ref[:, f + lo:f + hi], preferred_element_type=F32)
        part = jnp.dot((_silu(a) * g).astype(BF16), wd_ref[lo:hi, :], preferred_element_type=F32)
        y2 = part if y2 is None else y2 + part
    o_ref[...] =(_t3(x1) + m_ref[:, 5 * d:6 * d][None] * _t3(y2)).reshape(x.shape)


def _out_ffn(x, z, m, gf, wo, wgu, wd, tm):
    rows, d = x.shape
    kz = z.shape[-1]
    f = wd.shape[0]
    if z.ndim == 3:
        z_spec = pl.BlockSpec((SUBLANES, tm // SUBLANES, kz), lambda i: (0, i, 0))
        scratch = [pltpu.VMEM((kz // LANES, tm, LANES), F32)]
    else:
        z_spec = pl.BlockSpec((tm, kz), lambda i: (i, 0))
        scratch = []
    return pl.pallas_call(
        _out_ffn_kernel,
        grid=(rows // tm,),
        scratch_shapes=scratch,
        in_specs=[
            pl.BlockSpec((tm, d), lambda i: (i, 0)),
            z_spec,
            _const_spec((SUBLANES, 6 * d)),
            _const_spec((1, d)),
            _const_spec((kz, d)),
            _const_spec((d, 2 * f)),
            _const_spec((f, d)),
        ],
        out_specs=pl.BlockSpec((tm, d), lambda i: (i, 0)),
        out_shape=jax.ShapeDtypeStruct((rows, d), F32),
        compiler_params=_cparams(("arbitrary",)),
        name="out_ffn",
    )(x, z, m, gf, wo, wgu, wd)


def _inproj_lru_kernel(x_ref, m_ref, g_ref, w_ref, gate_ref, u_ref):
    d = x_ref.shape[1]
    c = u_ref.shape[1]
    h = _norm_mod(x_ref[...], g_ref[...], m_ref[:, 0:d], m_ref[:, d:2 * d])
    y = jnp.dot(h.astype(BF16), w_ref[...], preferred_element_type=F32)
    gate_ref[...] = _gelu_tanh(y[:, :c]).astype(BF16)
    u_ref[...] = y[:, c:]


def _inproj_lru(x, m, g, w_in, tm):
    rows, d = x.shape
    c = w_in.shape[1] // 2
    return pl.pallas_call(
        _inproj_lru_kernel,
        grid=(rows // tm,),
        in_specs=[
            pl.BlockSpec((tm, d), lambda i: (i, 0)),
            _const_spec((SUBLANES, 6 * d)),
            _const_spec((1, d)),
            _const_spec((d, 2 * c)),
        ],
        out_specs=[pl.BlockSpec((tm, c), lambda i: (i, 0)),
                   pl.BlockSpec((tm, c), lambda i: (i, 0))],
        out_shape=[jax.ShapeDtypeStruct((rows, c), BF16),
                   jax.ShapeDtypeStruct((rows, c), F32)],
        compiler_params=_cparams(("arbitrary",)),
        name="inproj_lru",
    )(x, m, g, w_in)


def _lru_scan_kernel(*refs, nblk, reverse, combine):
    if combine:
        (u_ref, up_ref, un_ref, cw_ref, cb_ref, gw_ref, gb_ref, lam_ref, h0_ref,
         hother_ref, gate_ref, out_ref, hlast_ref, ext_scr, a_scr, b_scr, h_scr) = refs
    else:
        (u_ref, up_ref, un_ref, cw_ref, cb_ref, gw_ref, gb_ref, lam_ref, h0_ref,
         out_ref, hlast_ref, ext_scr, a_scr, b_scr, h_scr) = refs
    i = pl.program_id(0)
    j = (nblk - 1 - i) if reverse else i
    rows, c = u_ref.shape
    steps = rows // SUBLANES
    halo_l = CONV_LEFT * SUBLANES
    halo_r = (CONV_W - 1 - CONV_LEFT) * SUBLANES
    nblocks = c // LRU_BLOCK_W

    @pl.when(i == 0)
    def _():
        h_scr[...] = h0_ref[...]

    ext_scr[0:halo_l, :] = jnp.where(j > 0, up_ref[...], 0.0)
    ext_scr[halo_l:halo_l + rows, :] = u_ref[...]
    ext_scr[halo_l + rows:, :] = jnp.where(j < nblk - 1, un_ref[...], 0.0)

    for n in range(nblocks):
        ls = slice(n * LRU_BLOCK_W, (n + 1) * LRU_BLOCK_W)
        u = cb_ref[:, ls] + cw_ref[0:1, ls] * ext_scr[0:rows, ls]
        for k in range(1, CONV_W):
            u = u + cw_ref[k:k + 1, ls] * ext_scr[k * SUBLANES:k * SUBLANES + rows, ls]
        z = jnp.dot(u.astype(BF16), gw_ref[n], preferred_element_type=F32) + gb_ref[n]
        tr = jnp.tanh(z[:, :LRU_BLOCK_W])
        ti = jnp.tanh(z[:, LRU_BLOCK_W:])
        lam = lam_ref[:, ls]
        softplus_neg = jnp.maximum(-lam, 0.0) + jnp.log1p(jnp.exp(-jnp.abs(lam)))
        half_c = (-0.5 * LRU_C) * softplus_neg
        log_a = half_c * tr + half_c
        a = jnp.exp(log_a)
        one_m_a2 = jnp.tanh(log_a) * (-1.0 - a * a)
        root = jnp.where(one_m_a2 > 0.0, one_m_a2 * lax.rsqrt(one_m_a2), 0.0)
        a_scr[:, ls] = a
        b_scr[:, ls] = root * ((ti + 1.0) * u)

    def step(s, h):
        t = (steps - 1 - s) if reverse else s
        off = pl.multiple_of(t * SUBLANES, SUBLANES)
        h = a_scr[pl.ds(off, SUBLANES), :] * h + b_scr[pl.ds(off, SUBLANES), :]
        b_scr[pl.ds(off, SUBLANES), :] = h
        return h

    h_fin = lax.fori_loop(0, steps, step, h_scr[...])
    h_scr[...] = h_fin
    hlast_ref[...] = h_fin
    if combine:
        out_ref[...] = ((hother_ref[...] + b_scr[...]) * gate_ref[...].astype(F32)).astype(out_ref.dtype)
    else:
        out_ref[...] = b_scr[...]


def _lru_scan(u, conv_w, conv_b, gw, gb, lam, h0, *, reverse, tl, h_other=None, gate=None):
    rows_total, c = u.shape
    rows = tl * SUBLANES
    nblk = rows_total // rows
    halo_l = CONV_LEFT * SUBLANES
    halo_r = (CONV_W - 1 - CONV_LEFT) * SUBLANES
    combine = h_other is not None
    nblocks = c // LRU_BLOCK_W

    def blk(i):
        return (nblk - 1 - i) if reverse else i

    in_specs = [
        pl.BlockSpec((rows, c), lambda i: (blk(i), 0)),
        pl.BlockSpec((halo_l, c), lambda i: (jnp.maximum(blk(i) * (rows // halo_l) - 1, 0), 0)),
        pl.BlockSpec((halo_r, c), lambda i: (jnp.minimum((blk(i) + 1) * (rows // halo_r),
                                                         rows_total // halo_r - 1), 0)),
        _const_spec((CONV_W, c)),
        _const_spec((1, c)),
        _const_spec((nblocks, LRU_BLOCK_W, 2 * LRU_BLOCK_W)),
        _const_spec((nblocks, 1, 2 * LRU_BLOCK_W)),
        _const_spec((1, c)),
        _const_spec((SUBLANES, c)),
    ]
    args = [u, u, u, conv_w, conv_b, gw, gb, lam, h0]
    if combine:
        in_specs += [pl.BlockSpec((rows, c), lambda i: (blk(i), 0)),
                     pl.BlockSpec((rows, c), lambda i: (blk(i), 0))]
        args += [h_other, gate]
    out_dtype = BF16 if combine else F32
    return pl.pallas_call(
        functools.partial(_lru_scan_kernel, nblk=nblk, reverse=reverse, combine=combine),
        grid=(nblk,),
        in_specs=in_specs,
        out_specs=[pl.BlockSpec((rows, c), lambda i: (blk(i), 0)),
                   pl.BlockSpec((SUBLANES, c), lambda i: (0, 0))],
        out_shape=[jax.ShapeDtypeStruct((rows_total, c), out_dtype),
                   jax.ShapeDtypeStruct((SUBLANES, c), F32)],
        scratch_shapes=[
            pltpu.VMEM((halo_l + rows + halo_r, c), F32),
            pltpu.VMEM((rows, c), F32),
            pltpu.VMEM((rows, c), F32),
            pltpu.VMEM((SUBLANES, c), F32),
        ],
        compiler_params=_cparams(("arbitrary",)),
        name="lru_scan_bwd" if reverse else "lru_scan_fwd",
    )(*args)


def _lru_mixer(xc, xl, mc, ml, g, p, need_ctx, tm, tl):
    w_in, conv_w, conv_b, gw, gb, lam = p
    c = w_in.shape[1] // 2
    gate_c, u_c = _inproj_lru(xc, mc, g, w_in, tm)
    gate_l, u_l = _inproj_lru(xl, ml, g, w_in, tm)
    h0 = jnp.zeros((SUBLANES, c), F32)
    tl_c = min(tl, xc.shape[0] // SUBLANES)
    hf_c, s_c = _lru_scan(u_c, conv_w, conv_b, gw[0], gb[0], lam[0], h0, reverse=False, tl=tl_c)
    hf_l, _ = _lru_scan(u_l, conv_w, conv_b, gw[0], gb[0], lam[0], s_c, reverse=False, tl=tl)
    if need_ctx:
        z_c, s_c = _lru_scan(u_c, conv_w, conv_b, gw[1], gb[1], lam[1], h0, reverse=True, tl=tl_c,
                             h_other=hf_c, gate=gate_c)
    else:
        z_c = None
        _, s_c = _lru_scan(u_c, conv_w, conv_b, gw[1], gb[1], lam[1], h0, reverse=True, tl=tl_c)
    z_l, _ = _lru_scan(u_l, conv_w, conv_b, gw[1], gb[1], lam[1], s_c, reverse=True, tl=tl,
                       h_other=hf_l, gate=gate_l)
    return z_c, z_l


def _prep_lru(w_in, conv_w, conv_b, gate_w, gate_b, lam):
    nb = gate_w.shape[2]
    gw = jnp.concatenate([gate_w[:, 0], gate_w[:, 1]], axis=-1).astype(BF16)
    gb = jnp.concatenate([gate_b[:, 0].reshape(2, nb, 1, LRU_BLOCK_W),
                          gate_b[:, 1].reshape(2, nb, 1, LRU_BLOCK_W)], axis=-1)
    return (w_in.astype(BF16), 0.5 * conv_w, 0.5 * conv_b.reshape(1, -1), gw, 0.5 * gb,
            lam.reshape(2, 1, -1))


NA_HEAD_DIM = 64
GRID_W = 64
WIN_ROWS = 8
WIN_COLS = 16
NEG = -1e30
LOG2E = 1.4426950408889634


def _inproj_na_kernel(x_ref, m_ref, g_ref, w_ref, qg_ref, kg_ref, ones_ref, q_ref, k_ref, v_ref, y_scr):
    d = x_ref.shape[1]
    tl = x_ref.shape[0] // SUBLANES
    h = _norm_mod(x_ref[...], g_ref[...], m_ref[:, 0:d], m_ref[:, d:2 * d])
    y = jnp.dot(h.astype(BF16), w_ref[...], preferred_element_type=F32)
    cw = ones_ref.shape[0]
    for part, gain_ref in ((0, qg_ref), (1, kg_ref)):
        for j in range(d // cw):
            lo = part * d + j * cw
            t = y[:, lo:lo + cw]
            ss = jnp.dot((t * t).astype(BF16), ones_ref[...], preferred_element_type=F32)
            tn = t * lax.rsqrt(ss * (1.0 / NA_HEAD_DIM) + EPS) * gain_ref[:, j * cw:(j + 1) * cw]
            for s in range(cw // LANES):
                y_scr[lo // LANES + s] = tn[:, s * LANES:(s + 1) * LANES]
    for s in range(d // LANES):
        y_scr[2 * d // LANES + s] = y[:, 2 * d + s * LANES:2 * d + (s + 1) * LANES]
    for b in range(SUBLANES):
        for part, o_ref in ((0, q_ref), (1, k_ref), (2, v_ref)):
            for s in range(d // LANES):
                o_ref[b, :, s * LANES:(s + 1) * LANES] = (
                    y_scr[part * (d // LANES) + s, pl.ds(b, tl, stride=SUBLANES), :].astype(BF16))


def _inproj_na(x, m, g, w_qkv, qg, kg, ones_bd, tm):
    rows, d = x.shape
    t_total = rows // SUBLANES
    tl = tm // SUBLANES
    out = jax.ShapeDtypeStruct((SUBLANES, t_total, d), BF16)
    ospec = pl.BlockSpec((SUBLANES, tl, d), lambda i: (0, i, 0))
    return pl.pallas_call(
        _inproj_na_kernel,
        grid=(rows // tm,),
        in_specs=[
            pl.BlockSpec((tm, d), lambda i: (i, 0)),
            _const_spec((SUBLANES, 6 * d)),
            _const_spec((1, d)),
            _const_spec((d, 3 * d)),
            _const_spec((1, d)),
            _const_spec((1, d)),
            _const_spec(ones_bd.shape),
        ],
        out_specs=[ospec, ospec, ospec],
        out_shape=[out, out, out],
        scratch_shapes=[pltpu.VMEM((3 * d // LANES, tm, LANES), F32)],
        compiler_params=_cparams(("arbitrary",)),
        name="inproj_na",
    )(x, m, g, w_qkv, qg, kg, ones_bd)


def _rpb_expand_kernel(rpb_ref, o_ref):
    h = pl.program_id(0)
    n_ro, n_co = rpb_ref.shape[1], rpb_ref.shape[2]
    c = lax.broadcasted_iota(jnp.int32, (GRID_W, GRID_W), 0)
    q = lax.broadcasted_iota(jnp.int32, (GRID_W, GRID_W), 1)
    cs = jnp.clip(q - WIN_COLS // 2, 0, GRID_W - WIN_COLS)
    valid = (c >= cs) & (c < cs + WIN_COLS)
    off = c - q + (WIN_COLS - 1)
    for ro in range(n_ro):
        acc = jnp.full((GRID_W, GRID_W), NEG, F32)
        for j in range(n_co):
            acc = jnp.where(off == j, rpb_ref[h, ro, j] * LOG2E, acc)
        o_ref[0, ro] = jnp.where(valid, acc, NEG)


def _rpb_tables(rpb):
    nh, n_ro, _ = rpb.shape
    m = pl.pallas_call(
        _rpb_expand_kernel,
        grid=(nh,),
        in_specs=[pl.BlockSpec(memory_space=pltpu.SMEM)],
        out_specs=pl.BlockSpec((1, n_ro, GRID_W, GRID_W), lambda h: (h, 0, 0, 0)),
        out_shape=jax.ShapeDtypeStruct((nh, n_ro, GRID_W, GRID_W), F32),
        compiler_params=_cparams(("arbitrary",)),
        name="rpb_expand",
    )(rpb)
    win = jnp.stack([m[:, r0:r0 + WIN_ROWS] for r0 in range(n_ro - WIN_ROWS + 1)])
    win = win.reshape(win.shape[0], nh // 2, 2, WIN_ROWS, GRID_W, GRID_W)
    win = win.transpose(0, 1, 3, 4, 2, 5)
    return win.reshape(win.shape[0], nh // 2, WIN_ROWS * GRID_W, 2 * GRID_W)


def _attn_pair(qp, parts):
    return _attn_finish(_attn_scores(qp, [(k, bias) for k, _, bias in parts]),
                        [v for _, v, _ in parts], qp.shape[0])


def _attn_scores(qp, parts):
    lane = lax.broadcasted_iota(jnp.int32, qp.shape, 1)
    zero = jnp.zeros_like(qp)
    qbd = jnp.concatenate([jnp.where(lane < NA_HEAD_DIM, qp, zero),
                           jnp.where(lane >= NA_HEAD_DIM, qp, zero)], axis=0)
    nt = (((1,), (1,)), ((), ()))
    scores = []
    for k, bias in parts:
        s = lax.dot_general(k, qbd, nt, preferred_element_type=F32)
        scores.append(s if bias is None else s + bias)
    return scores


def _attn_finish(scores, vs, nq):
    mx = functools.reduce(jnp.maximum, [jnp.max(s, axis=0, keepdims=True) for s in scores])
    es = [jnp.exp2(s - mx) for s in scores]
    den = functools.reduce(jnp.add, [jnp.sum(e, axis=0, keepdims=True) for e in es])
    tn = (((0,), (0,)), ((), ()))
    rt = functools.reduce(jnp.add, [
        lax.dot_general(v, e.astype(BF16), tn, preferred_element_type=F32)
        for e, v in zip(es, vs)])
    r = jnp.transpose(rt * (1.0 / den))
    lane_o = lax.broadcasted_iota(jnp.int32, (nq, LANES), 1)
    return jnp.where(lane_o < NA_HEAD_DIM, r[:nq], r[nq:])


def _na_attn_kernel(q_ref, k_ref, v_ref, kc_ref, vc_ref, bias_ref, o_ref, *, rows):
    r = pl.program_id(1)
    rs = jnp.clip(r - WIN_ROWS // 2, 0, rows - WIN_ROWS)
    start = pl.multiple_of(rs * GRID_W, GRID_W)
    nwin = WIN_ROWS * GRID_W
    npairs = q_ref.shape[2] // LANES

    def scores_of(p):
        ls = slice(p * LANES, (p + 1) * LANES)
        return _attn_scores(q_ref[0, :, ls], [(k_ref[0, pl.ds(start, nwin), ls], bias_ref[0, p]),
                                              (kc_ref[0, :, ls], None)])

    nxt = scores_of(0)
    for p in range(npairs):
        ls = slice(p * LANES, (p + 1) * LANES)
        cur = nxt
        if p + 1 < npairs:
            nxt = scores_of(p + 1)
        o = _attn_finish(cur, [v_ref[0, pl.ds(start, nwin), ls], vc_ref[0, :, ls]], q_ref.shape[1])
        o_ref[0, :, ls] = o.astype(o_ref.dtype)


def _na_attn(q, k, v, kc, vc, bias_t):
    b, l, d = q.shape
    ct = kc.shape[1]
    rows = l // GRID_W
    half = WIN_ROWS // 2

    def bias_idx(bi, r):
        rs = jnp.clip(r - half, 0, rows - WIN_ROWS)
        return (rs - r + (WIN_ROWS - 1), 0, 0, 0)

    return pl.pallas_call(
        functools.partial(_na_attn_kernel, rows=rows),
        grid=(b, rows),
        in_specs=[
            pl.BlockSpec((1, GRID_W, d), lambda bi, r: (bi, r, 0)),
            pl.BlockSpec((1, l, d), lambda bi, r: (bi, 0, 0), pipeline_mode=pl.Buffered(1)),
            pl.BlockSpec((1, l, d), lambda bi, r: (bi, 0, 0), pipeline_mode=pl.Buffered(1)),
            pl.BlockSpec((1, ct, d), lambda bi, r: (bi, 0, 0)),
            pl.BlockSpec((1, ct, d), lambda bi, r: (bi, 0, 0)),
            pl.BlockSpec((1,) + bias_t.shape[1:], bias_idx),
        ],
        out_specs=pl.BlockSpec((1, GRID_W, d), lambda bi, r: (bi, r, 0)),
        out_shape=jax.ShapeDtypeStruct((b, l, d), BF16),
        compiler_params=_cparams(("arbitrary", "arbitrary")),
        name="na_attn",
    )(q, k, v, kc, vc, bias_t)


def _ctx_attn_kernel(q_ref, k_ref, v_ref, o_ref):
    for p in range(q_ref.shape[2] // LANES):
        ls = slice(p * LANES, (p + 1) * LANES)
        o = _attn_pair(q_ref[0, :, ls], [(k_ref[0, :, ls], v_ref[0, :, ls], None)])
        o_ref[0, :, ls] = o.astype(o_ref.dtype)


def _ctx_attn(q, k, v):
    b, ct, d = q.shape
    spec = pl.BlockSpec((1, ct, d), lambda bi: (bi, 0, 0))
    return pl.pallas_call(
        _ctx_attn_kernel,
        grid=(b,),
        in_specs=[spec, spec, spec],
        out_specs=spec,
        out_shape=jax.ShapeDtypeStruct((b, ct, d), BF16),
        compiler_params=_cparams(("arbitrary",)),
        name="ctx_attn",
    )(q, k, v)


def _prep_na(w_qkv, qg, kg):
    d = w_qkv.shape[0]
    nh = d // NA_HEAD_DIM
    head = jnp.arange(2 * LANES) // NA_HEAD_DIM
    ones_bd = (head[:, None] == head[None, :]).astype(BF16)
    qg_t = jnp.tile(qg, nh).reshape(1, d) * (NA_HEAD_DIM ** -0.5 * LOG2E)
    kg_t = jnp.tile(kg, nh).reshape(1, d)
    return w_qkv.astype(BF16), qg_t, kg_t, ones_bd


def _na_mixer(xc, xl, mc, ml, g, p, bias_t, need_ctx, tm):
    w_qkv, qg_t, kg_t, ones_bd = p
    q_c, k_c, v_c = _inproj_na(xc, mc, g, w_qkv, qg_t, kg_t, ones_bd, tm)
    q_l, k_l, v_l = _inproj_na(xl, ml, g, w_qkv, qg_t, kg_t, ones_bd, tm)
    o_l = _na_attn(q_l, k_l, v_l, k_c, v_c, bias_t)
    o_c = _ctx_attn(q_c, k_c, v_c) if need_ctx else None
    return o_c, o_l


HG_DK = 128
HG_CHUNK = 64
HG_SUB = 16
HG_HEADS_PER_STEP = 2
HG_SAFE_LOG2 = 100.0


def _lb_kernel(x_ref, o_ref):
    x = x_ref[...]
    e = jnp.exp(x - jnp.max(x, axis=0, keepdims=True))
    p = e / jnp.sum(e, axis=0, keepdims=True)
    acc = jnp.zeros_like(p[0:1])
    for l in range(x.shape[0]):
        o_ref[l:l + 1, :] = acc
        if l + 1 < x.shape[0]:
            acc = acc + p[l + 1:l + 2]


def _lower_bounds(lb_logits):
    return pl.pallas_call(
        _lb_kernel,
        out_shape=jax.ShapeDtypeStruct(lb_logits.shape, F32),
        name="hg_lower_bounds",
    )(lb_logits)


def _inproj_hg_kernel(x_ref, m_ref, g_ref, w_ref, q_ref, v_ref, gs_ref, zf_ref, zb_ref):
    d = x_ref.shape[1]
    h = _norm_mod(x_ref[...], g_ref[...], m_ref[:, 0:d], m_ref[:, d:2 * d])
    y = jnp.dot(h.astype(BF16), w_ref[...], preferred_element_type=F32)
    q_ref[...] = _silu(y[:, 0:d]).astype(BF16)
    v_ref[...] = y[:, d:2 * d].astype(BF16)
    gs_ref[...] = _silu(y[:, 2 * d:3 * d]).astype(BF16)
    zf_ref[...] = y[:, 3 * d:4 * d]
    zb_ref[...] = y[:, 4 * d:5 * d]


def _inproj_hg(x, m, g, w_in, tm):
    rows, d = x.shape
    spec = pl.BlockSpec((tm, d), lambda i: (i, 0))
    return pl.pallas_call(
        _inproj_hg_kernel,
        grid=(rows // tm,),
        in_specs=[spec, _const_spec((SUBLANES, 6 * d)), _const_spec((1, d)), _const_spec((d, 5 * d))],
        out_specs=[spec] * 5,
        out_shape=[jax.ShapeDtypeStruct((rows, d), BF16)] * 3 + [jax.ShapeDtypeStruct((rows, d), F32)] * 2,
        compiler_params=_cparams(("arbitrary",)),
        name="inproj_hg",
    )(x, m, g, w_in)


def _hg_scan_kernel(*refs, reverse, combine):
    nin = 8 if combine else 5
    if combine:
        q_ref, v_ref, z_ref, lb_ref, s0_ref, oo_ref, gs_ref, ng_ref = refs[:nin]
    else:
        q_ref, v_ref, z_ref, lb_ref, s0_ref = refs[:nin]
        oo_ref = gs_ref = ng_ref = None
    out_ref, sout_ref, s_scr = refs[nin:nin + 3]
    hp = q_ref.shape[1] // LANES
    nwork = 7
    work = [refs[nin + 3 + nwork * hh:nin + 3 + nwork * (hh + 1)] for hh in range(hp)]
    h = pl.program_id(0)
    i = pl.program_id(1)

    @pl.when(i == 0)
    def _():
        s_scr[...] = s0_ref[...]

    @pl.when((i == 0) & (h == 0))
    def _():
        for qe_scr, ke_scr, *_ in work:
            qe_scr[...] = jnp.zeros_like(qe_scr)
            ke_scr[...] = jnp.zeros_like(ke_scr)

    for hh in range(hp):
        def cols(ref):
            return None if ref is None else ref.at[:, hh * LANES:(hh + 1) * LANES]
        _hg_head(cols(q_ref), cols(v_ref), cols(z_ref), cols(lb_ref), cols(oo_ref), cols(gs_ref), ng_ref,
                 cols(out_ref), sout_ref.at[hh], s_scr.at[hh], *work[hh], reverse, combine)


def _hg_head(q_ref, v_ref, z_ref, lb_ref, oo_ref, gs_ref, ng_ref, out_ref, sout_ref, s_scr, qe_scr,
             ke_scr, qt_scr, kt_scr, kh_scr, v_scr, o_scr, reverse, combine):
    nsteps = HG_CHUNK
    nsub = nsteps // HG_SUB
    rows = nsteps * SUBLANES
    order = range(nsteps - 1, -1, -1) if reverse else range(nsteps)

    def load_steps(ref):
        x3 = ref[...].astype(F32).reshape(nsteps, SUBLANES, LANES)
        return jnp.stack([x3[t] for t in order]) if reverse else x3

    def store_steps(val3):
        return jnp.concatenate([val3[s] for s in order], axis=0)

    lb = lb_ref[...]
    q = load_steps(q_ref)
    v = load_steps(v_ref)
    sg = _sigmoid(load_steps(z_ref))
    k = (1.0 - lb) * (1.0 - sg)
    logf = jnp.log(lb + (1.0 - lb) * sg) * LOG2E
    acc = logf[0]
    cum = [acc]
    for s in range(1, nsteps):
        acc = acc + logf[s]
        cum.append(acc)
    b = jnp.stack(cum)
    b_end = cum[-1]

    v_scr[...] = v.reshape(rows, LANES)
    qt_scr[...] = (q * jnp.exp2(b)).reshape(rows, LANES)
    kt_scr[...] = (k * jnp.exp2(b_end[None] - b)).reshape(rows, LANES)

    nt = (((1,), (1,)), ((), ()))
    tn = (((0,), (0,)), ((), ()))
    decay_end = jnp.exp2(b_end)

    def batch_updates(a_intra):
        results = []
        for bb in range(SUBLANES):
            sl = pl.ds(bb, nsteps, stride=SUBLANES)
            vb = v_scr[sl, :].astype(BF16)
            st = s_scr[bb]
            o_b = (jnp.dot(a_intra[bb].astype(BF16), vb, preferred_element_type=F32)
                   + lax.dot_general(qt_scr[sl, :].astype(BF16), st.astype(BF16), nt,
                                     preferred_element_type=F32))
            s_new = st * decay_end[bb:bb + 1, :] + lax.dot_general(
                vb, kt_scr[sl, :].astype(BF16), tn, preferred_element_type=F32)
            results.append((o_b, s_new))
        for bb, (o_b, s_new) in enumerate(results):
            o_scr[pl.ds(bb, nsteps, stride=SUBLANES), :] = o_b
            s_scr[bb] = s_new
            sout_ref[bb] = s_new

    safe = jnp.min(b_end) > -HG_SAFE_LOG2

    @pl.when(safe)
    def _():
        kh_scr[...] = (k * jnp.exp2(-b)).reshape(rows, LANES)
        t_idx = lax.broadcasted_iota(jnp.int32, (nsteps, nsteps), 0)
        s_idx = lax.broadcasted_iota(jnp.int32, (nsteps, nsteps), 1)
        a_all = []
        for bb in range(SUBLANES):
            sl = pl.ds(bb, nsteps, stride=SUBLANES)
            a = lax.dot_general(qt_scr[sl, :].astype(BF16), kh_scr[sl, :].astype(BF16), nt,
                                preferred_element_type=F32)
            a_all.append(jnp.where(s_idx <= t_idx, a, 0.0))
        batch_updates(a_all)

    @pl.when(jnp.logical_not(safe))
    def _():
        for blk in range(1, nsub):
            lo = blk * HG_SUB
            beta = cum[lo - 1][None]
            qe_scr[blk - 1, lo * SUBLANES:(lo + HG_SUB) * SUBLANES, :] = (
                q[lo:lo + HG_SUB] * jnp.exp2(b[lo:lo + HG_SUB] - beta)).reshape(HG_SUB * SUBLANES, LANES)
            ke_scr[blk - 1, 0:lo * SUBLANES, :] = (
                k[:lo] * jnp.exp2(beta - b[:lo])).reshape(lo * SUBLANES, LANES)
        a_all = []
        for bb in range(SUBLANES):
            sl = pl.ds(bb, nsteps, stride=SUBLANES)
            qe = jnp.concatenate([qe_scr[j, sl, :] for j in range(nsub - 1)], axis=1).astype(BF16)
            ke = jnp.concatenate([ke_scr[j, sl, :] for j in range(nsub - 1)], axis=1).astype(BF16)
            a_all.append(lax.dot_general(qe, ke, nt, preferred_element_type=F32))
        batch_updates(a_all)
        shp = (nsub, HG_SUB, SUBLANES, LANES)
        b4, q4, k4, v4 = b.reshape(shp), q.reshape(shp), k.reshape(shp), v.reshape(shp)
        near0 = jnp.sum(q4 * k4, axis=-1, keepdims=True) * v4
        near = [near0[:, t] for t in range(HG_SUB)]
        for dlt in range(1, HG_SUB):
            n = HG_SUB - dlt
            w = q4[:, dlt:] * k4[:, :n] * jnp.exp2(b4[:, dlt:] - b4[:, :n])
            contrib = jnp.sum(w, axis=-1, keepdims=True) * v4[:, :n]
            for j in range(n):
                near[dlt + j] = near[dlt + j] + contrib[:, j]
        o_scr[...] = o_scr[...] + jnp.stack(near, axis=1).reshape(rows, LANES)

    o_tot = store_steps(o_scr[...].reshape(nsteps, SUBLANES, LANES))
    if combine:
        tot = oo_ref[...] + o_tot
        y = tot * lax.rsqrt(jnp.mean(tot * tot, axis=-1, keepdims=True) + EPS) * ng_ref[...]
        out_ref[...] = (y * gs_ref[...].astype(F32)).astype(out_ref.dtype)
    else:
        out_ref[...] = o_tot


def _hg_scan(q, v, z, lb, s0, *, reverse, o_other=None, gs=None, norm_g=None):
    rows_total, d = q.shape
    nh = d // HG_DK
    rows = HG_CHUNK * SUBLANES
    nblk = rows_total // rows
    combine = o_other is not None

    def blk(i):
        return (nblk - 1 - i) if reverse else i

    hp = HG_HEADS_PER_STEP
    tile = pl.BlockSpec((rows, hp * HG_DK), lambda h, i: (blk(i), h))
    state = pl.BlockSpec((hp, SUBLANES, HG_DK, HG_DK), lambda h, i: (h, 0, 0, 0))
    in_specs = [tile, tile, tile, pl.BlockSpec((1, hp * HG_DK), lambda h, i: (0, h)), state]
    args = [q, v, z, lb, s0]
    if combine:
        in_specs += [tile, tile, pl.BlockSpec((1, HG_DK), lambda h, i: (0, 0))]
        args += [o_other, gs, norm_g]
    nref = HG_CHUNK // HG_SUB - 1
    return pl.pallas_call(
        functools.partial(_hg_scan_kernel, reverse=reverse, combine=combine),
        grid=(nh // hp, nblk),
        in_specs=in_specs,
        out_specs=[tile, state],
        out_shape=[jax.ShapeDtypeStruct((rows_total, d), BF16 if combine else F32),
                   jax.ShapeDtypeStruct(s0.shape, F32)],
        scratch_shapes=[pltpu.VMEM((hp, SUBLANES, HG_DK, HG_DK), F32)] + hp * [
            pltpu.VMEM((nref, rows, LANES), F32),
            pltpu.VMEM((nref, rows, LANES), F32),
            pltpu.VMEM((rows, LANES), F32),
            pltpu.VMEM((rows, LANES), F32),
            pltpu.VMEM((rows, LANES), F32),
            pltpu.VMEM((rows, LANES), F32),
            pltpu.VMEM((rows, LANES), F32),
        ],
        compiler_params=_cparams(("arbitrary", "arbitrary")),
        name="hg_scan_bwd" if reverse else "hg_scan_fwd",
    )(*args)


def _hg_mixer(xc, xl, mc, ml, g, w_in, lb, norm_g, need_ctx, tm):
    d = xl.shape[1]
    nh = d // HG_DK
    q_c, v_c, gs_c, zf_c, zb_c = _inproj_hg(xc, mc, g, w_in, tm)
    q_l, v_l, gs_l, zf_l, zb_l = _inproj_hg(xl, ml, g, w_in, tm)
    s0 = jnp.zeros((nh, SUBLANES, HG_DK, HG_DK), F32)
    of_c, s_c = _hg_scan(q_c, v_c, zf_c, lb, s0, reverse=False)
    of_l, _ = _hg_scan(q_l, v_l, zf_l, lb, s_c, reverse=False)
    if need_ctx:
        z_c, s_c = _hg_scan(q_c, v_c, zb_c, lb, s0, reverse=True, o_other=of_c, gs=gs_c, norm_g=norm_g)
    else:
        z_c = None
        _, s_c = _hg_scan(q_c, v_c, zb_c, lb, s0, reverse=True)
    z_l, _ = _hg_scan(q_l, v_l, zb_l, lb, s_c, reverse=True, o_other=of_l, gs=gs_l, norm_g=norm_g)
    return z_c, z_l


def kernel(x, c, ctx, c_ctx, mod_w, mod_b, norm_mix_g, norm_ffn_g, ffn_w_gu, ffn_w_down, lru_w_in, lru_conv_w, lru_conv_b, lru_gate_w, lru_gate_b, lru_lambda, lru_w_out, na_w_qkv, na_q_norm_g, na_k_norm_g, na_rpb, na_w_o, hg_w_in, hg_lb_logits, hg_norm_g, hg_w_o):
    b, l, d = x.shape
    ct = ctx.shape[1]
    depth = mod_w.shape[0]
    tm = ROW_TILE
    cc = jnp.zeros((2 * SUBLANES, d), F32).at[:b].set(c).at[b].set(c_ctx)
    mod = _modulation(cc, mod_w, mod_b)
    lower_bounds = _lower_bounds(hg_lb_logits)
    xl = x.transpose(1, 0, 2).reshape(l * b, d)
    xc = ctx.transpose(1, 0, 2).reshape(ct * b, d)
    for layer in range(depth):
        kind, slot = layer % 3, layer // 3
        need_ctx = layer < depth - 1
        ml = mod[layer, :b]
        mc = jnp.broadcast_to(mod[layer, b:b + 1], (b, 6 * d))
        g = norm_mix_g[layer].reshape(1, d)
        gf = norm_ffn_g[layer].reshape(1, d)
        wgu = ffn_w_gu[layer].astype(BF16)
        wd = ffn_w_down[layer].astype(BF16)
        if kind == 0:
            p = _prep_lru(lru_w_in[slot], lru_conv_w[slot], lru_conv_b[slot], lru_gate_w[slot],
                          lru_gate_b[slot], lru_lambda[slot])
            zc, zl = _lru_mixer(xc, xl, mc, ml, g, p, need_ctx, tm, LRU_STEPS)
            wo = lru_w_out[slot].astype(BF16)
        elif kind == 1:
            p = _prep_na(na_w_qkv[slot], na_q_norm_g[slot], na_k_norm_g[slot])
            zc, zl = _na_mixer(xc, xl, mc, ml, g, p, _rpb_tables(na_rpb[slot]), need_ctx, tm)
            wo = na_w_o[slot].astype(BF16)
        else:
            zc, zl = _hg_mixer(xc, xl, mc, ml, g, hg_w_in[slot].astype(BF16),
                               lower_bounds[layer:layer + 1], hg_norm_g[slot].reshape(1, -1), need_ctx, tm)
            wo = hg_w_o[slot].astype(BF16)
        xl = _out_ffn(xl, zl, ml, gf, wo, wgu, wd, tm)
        if need_ctx:
            xc = _out_ffn(xc, zc, mc, gf, wo, wgu, wd, tm)
    return xl.reshape(l, b, d).transpose(1, 0, 2)
```

```python
import functools

import jax
import jax.numpy as jnp
from jax import lax
from jax.experimental import pallas as pl
from jax.experimental.pallas import tpu as pltpu

F32 = jnp.float32
BF16 = jnp.bfloat16

SUBLANES = 8
LANES = 128
EPS = 1e-6
LRU_C = 8.0
LRU_BLOCK_W = 128
CONV_W = 4
CONV_LEFT = 2
VMEM_LIMIT = 56 * 1024 * 1024
MXU_TILE = 256
FFN_CHUNKS = 2
ROW_TILE = 512
LRU_STEPS = 64


def _split_aligned(n, parts, align):
    if n % align:
        return [(0, n)]
    tiles = n // align
    cuts = [round(i * tiles / parts) * align for i in range(parts + 1)]
    return [(lo, hi) for lo, hi in zip(cuts[:-1], cuts[1:]) if hi > lo]


def _cparams(sem):
    return pltpu.CompilerParams(dimension_semantics=sem, vmem_limit_bytes=VMEM_LIMIT)


def _const_spec(shape):
    nd = len(shape)
    return pl.BlockSpec(shape, lambda *_: (0,) * nd, pipeline_mode=pl.Buffered(1))


def _sigmoid(x):
    return 0.5 * jnp.tanh(0.5 * x) + 0.5


def _silu(x):
    return x * _sigmoid(x)


def _gelu_tanh(x):
    return 0.5 * x * (1.0 + jnp.tanh(0.7978845608028654 * (x + 0.044715 * (x * x * x))))


def _t3(x):
    return x.reshape(x.shape[0] // SUBLANES, SUBLANES, x.shape[1])


def _norm_mod(x, g, shift, scale):
    ms = jnp.mean(x * x, axis=-1, keepdims=True)
    y = x * lax.rsqrt(ms + EPS) * g
    y3 = _t3(y) * (1.0 + scale)[None] + shift[None]
    return y3.reshape(x.shape)


def _mod_kernel(c_ref, w_ref, b_ref, o_ref):
    a = _silu(c_ref[...])
    o_ref[0] = jnp.dot(a, w_ref[0], preferred_element_type=F32) + b_ref[0]


def _modulation(cc, mod_w, mod_b):
    depth, d, d6 = mod_w.shape
    nblk = d6 // d
    return pl.pallas_call(
        _mod_kernel,
        grid=(depth, nblk),
        in_specs=[
            pl.BlockSpec((2 * SUBLANES, d), lambda l, j: (0, 0)),
            pl.BlockSpec((1, d, d), lambda l, j: (l, 0, j)),
            pl.BlockSpec((1, 1, d), lambda l, j: (l, 0, j)),
        ],
        out_specs=pl.BlockSpec((1, 2 * SUBLANES, d), lambda l, j: (l, 0, j)),
        out_shape=jax.ShapeDtypeStruct((depth, 2 * SUBLANES, d6), F32),
        compiler_params=_cparams(("arbitrary", "arbitrary")),
        name="modulation",
    )(cc, mod_w, mod_b.reshape(depth, 1, d6))


def _rows_from_batches(ref3, scr):
    tl = ref3.shape[1]
    nslab = ref3.shape[2] // LANES
    for b in range(SUBLANES):
        for s in range(nslab):
            scr[s, pl.ds(b, tl, stride=SUBLANES), :] = ref3[b, :, s * LANES:(s + 1) * LANES].astype(F32)
    return jnp.concatenate([scr[s] for s in range(nslab)], axis=1)


def _rows_to_batches(val, ref3, scr):
    tl = ref3.shape[1]
    nslab = ref3.shape[2] // LANES
    for s in range(nslab):
        scr[s] = val[:, s * LANES:(s + 1) * LANES]
    for b in range(SUBLANES):
        for s in range(nslab):
            ref3[b, :, s * LANES:(s + 1) * LANES] = scr[s, pl.ds(b, tl, stride=SUBLANES), :].astype(ref3.dtype)


def _slab_scratch(tm, cols):
    return pltpu.VMEM((cols // LANES, tm, LANES), F32)


def _out_ffn_kernel(x_ref, z_ref, m_ref, gf_ref, wo_ref, wgu_ref, wd_ref, o_ref, *scratch):
    d = m_ref.shape[1] // 6
    f = wd_ref.shape[0]
    scratch = list(scratch)
    x = _rows_from_batches(x_ref, scratch.pop(0)) if len(x_ref.shape) == 3 else x_ref[...]
    z = _rows_from_batches(z_ref, scratch.pop(0)).astype(BF16) if len(z_ref.shape) == 3 else z_ref[...]
    y = jnp.dot(z, wo_ref[...], preferred_element_type=F32)
    x1 = (_t3(x) + m_ref[:, 2 * d:3 * d][None] * _t3(y)).reshape(x.shape)
    h = _norm_mod(x1, gf_ref[...], m_ref[:, 3 * d:4 * d], m_ref[:, 4 * d:5 * d])
    hb = h.astype(BF16)
    y2 = None
    for lo, hi in _split_aligned(f, FFN_CHUNKS, MXU_TILE):
        a = jnp.dot(hb, wgu_ref[:, lo:hi], preferred_element_type=F32)
        g = jnp.dot(hb, wgu_ref[:, f + lo:f + hi], preferred_element_type=F32)
        part = jnp.dot((_silu(a) * g).astype(BF16), wd_ref[lo:hi, :], preferred_element_type=F32)
        y2 = part if y2 is None else y2 + part
    out = (_t3(x1) + m_ref[:, 5 * d:6 * d][None] * _t3(y2)).reshape(x.shape)
    if len(o_ref.shape) == 3:
        _rows_to_batches(out, o_ref, scratch.pop(0))
    else:
        o_ref[...] = out


def _row_spec(a, tm):
    if a.ndim == 3:
        return pl.BlockSpec((SUBLANES, tm // SUBLANES, a.shape[2]), lambda i: (0, i, 0))
    return pl.BlockSpec((tm, a.shape[1]), lambda i: (i, 0))


def _out_ffn(x, z, m, gf, wo, wgu, wd, tm, out_per_batch=False):
    d = x.shape[-1]
    rows = x.size // d
    kz = z.shape[-1]
    f = wd.shape[0]
    scratch = [_slab_scratch(tm, a.shape[-1]) for a in (x, z) if a.ndim == 3]
    if out_per_batch:
        scratch.append(_slab_scratch(tm, d))
        out_shape = jax.ShapeDtypeStruct((SUBLANES, rows // SUBLANES, d), F32)
    else:
        out_shape = jax.ShapeDtypeStruct((rows, d), F32)
    return pl.pallas_call(
        _out_ffn_kernel,
        grid=(rows // tm,),
        scratch_shapes=scratch,
        in_specs=[
            _row_spec(x, tm),
            _row_spec(z, tm),
            _const_spec((SUBLANES, 6 * d)),
            _const_spec((1, d)),
            _const_spec((kz, d)),
            _const_spec((d, 2 * f)),
            _const_spec((f, d)),
        ],
        out_specs=_row_spec(out_shape, tm),
        out_shape=out_shape,
        compiler_params=_cparams(("arbitrary",)),
        name="out_ffn",
    )(x, z, m, gf, wo, wgu, wd)


def _inproj_lru_kernel(x_ref, xp_ref, xn_ref, m_ref, g_ref, w_ref, cw_ref, cb_ref, gate_ref, u_ref,
                       *scratch, nblk):
    i = pl.program_id(0)
    d = m_ref.shape[1] // 6
    rows, c = u_ref.shape
    halo = LRU_HALO
    if scratch:
        xs = [_rows_from_batches(xp_ref, scratch[0])[-halo:], _rows_from_batches(x_ref, scratch[1]),
              _rows_from_batches(xn_ref, scratch[2])[:halo]]
    else:
        xs = [xp_ref[...], x_ref[...], xn_ref[...]]
    shift, scale = m_ref[:, 0:d], m_ref[:, d:2 * d]
    h = jnp.concatenate([_norm_mod(x, g_ref[...], shift, scale).astype(BF16) for x in xs],
                        axis=0)
    y = jnp.dot(h, w_ref[...], preferred_element_type=F32)
    gate_ref[...] = _gelu_tanh(y[halo:halo + rows, :c]).astype(BF16)
    u = y[:, c:]
    row = lax.broadcasted_iota(jnp.int32, (u.shape[0], 1), 0)
    inside = ((row >= halo) | (i > 0)) & ((row < halo + rows) | (i < nblk - 1))
    u = jnp.where(inside, u, 0.0)
    first = halo - CONV_LEFT * SUBLANES
    acc = cb_ref[...] + cw_ref[0:1, :] * u[first:first + rows]
    for k in range(1, CONV_W):
        acc = acc + cw_ref[k:k + 1, :] * u[first + k * SUBLANES:first + k * SUBLANES + rows]
    u_ref[...] = acc


LRU_HALO = 16


def _inproj_lru(x, m, g, w_in, conv_w, conv_b, tm):
    d = x.shape[-1]
    rows = x.size // d
    c = w_in.shape[1] // 2
    nblk = rows // tm
    if x.ndim == 3:
        per = tm // (SUBLANES * SUBLANES)
        last = rows // (SUBLANES * SUBLANES) - 1
        halo_specs = [
            pl.BlockSpec((SUBLANES, SUBLANES, d), lambda i: (0, jnp.maximum(i * per - 1, 0), 0)),
            pl.BlockSpec((SUBLANES, SUBLANES, d), lambda i: (0, jnp.minimum((i + 1) * per, last), 0))]
        scratch = [_slab_scratch(SUBLANES * SUBLANES, d), _slab_scratch(tm, d),
                   _slab_scratch(SUBLANES * SUBLANES, d)]
    else:
        per = tm // LRU_HALO
        last = rows // LRU_HALO - 1
        halo_specs = [
            pl.BlockSpec((LRU_HALO, d), lambda i: (jnp.maximum(i * per - 1, 0), 0)),
            pl.BlockSpec((LRU_HALO, d), lambda i: (jnp.minimum((i + 1) * per, last), 0))]
        scratch = []
    return pl.pallas_call(
        functools.partial(_inproj_lru_kernel, nblk=nblk),
        grid=(nblk,),
        scratch_shapes=scratch,
        in_specs=[
            _row_spec(x, tm),
            *halo_specs,
            _const_spec((SUBLANES, 6 * d)),
            _const_spec((1, d)),
            _const_spec((d, 2 * c)),
            _const_spec((CONV_W, c)),
            _const_spec((1, c)),
        ],
        out_specs=[pl.BlockSpec((tm, c), lambda i: (i, 0)),
                   pl.BlockSpec((tm, c), lambda i: (i, 0))],
        out_shape=[jax.ShapeDtypeStruct((rows, c), BF16),
                   jax.ShapeDtypeStruct((rows, c), F32)],
        compiler_params=_cparams(("arbitrary",)),
        name="inproj_lru",
    )(x, x, x, m, g, w_in, conv_w, conv_b)


def _lru_scan_kernel(*refs, reverse, combine):
    if combine:
        (u_ref, gw_ref, gb_ref, lam_ref, h0_ref, hother_ref, gate_ref,
         out_ref, hlast_ref, a_scr, b_scr, h_scr) = refs
    else:
        u_ref, gw_ref, gb_ref, lam_ref, h0_ref, out_ref, hlast_ref, a_scr, b_scr, h_scr = refs
    i = pl.program_id(0)
    rows, c = u_ref.shape
    steps = rows // SUBLANES
    nblocks = c // LRU_BLOCK_W

    @pl.when(i == 0)
    def _():
        h_scr[...] = h0_ref[...]

    for n in range(nblocks):
        ls = slice(n * LRU_BLOCK_W, (n + 1) * LRU_BLOCK_W)
        u = u_ref[:, ls]
        z = jnp.dot(u.astype(BF16), gw_ref[n], preferred_element_type=F32) + gb_ref[n]
        tr = jnp.tanh(z[:, :LRU_BLOCK_W])
        ti = jnp.tanh(z[:, LRU_BLOCK_W:])
        lam = lam_ref[:, ls]
        softplus_neg = jnp.maximum(-lam, 0.0) + jnp.log1p(jnp.exp(-jnp.abs(lam)))
        half_c = (-0.5 * LRU_C) * softplus_neg
        log_a = half_c * tr + half_c
        a = jnp.exp(log_a)
        one_m_a2 = jnp.tanh(log_a) * (-1.0 - a * a)
        root = jnp.where(one_m_a2 > 0.0, one_m_a2 * lax.rsqrt(one_m_a2), 0.0)
        a_scr[:, ls] = a
        b_scr[:, ls] = root * ((ti + 1.0) * u)

    def step(s, h):
        t = (steps - 1 - s) if reverse else s
        off = pl.multiple_of(t * SUBLANES, SUBLANES)
        h = a_scr[pl.ds(off, SUBLANES), :] * h + b_scr[pl.ds(off, SUBLANES), :]
        b_scr[pl.ds(off, SUBLANES), :] = h
        return h

    h_fin = lax.fori_loop(0, steps, step, h_scr[...])
    h_scr[...] = h_fin
    hlast_ref[...] = h_fin
    if combine:
        out_ref[...] = ((hother_ref[...] + b_scr[...]) * gate_ref[...].astype(F32)).astype(out_ref.dtype)
    else:
        out_ref[...] = b_scr[...]


def _lru_scan(u, gw, gb, lam, h0, *, reverse, tl, h_other=None, gate=None):
    rows_total, c = u.shape
    rows = tl * SUBLANES
    nblk = rows_total // rows
    combine = h_other is not None
    nblocks = c // LRU_BLOCK_W

    def blk(i):
        return (nblk - 1 - i) if reverse else i

    tile = pl.BlockSpec((rows, c), lambda i: (blk(i), 0))
    in_specs = [
        tile,
        _const_spec((nblocks, LRU_BLOCK_W, 2 * LRU_BLOCK_W)),
        _const_spec((nblocks, 1, 2 * LRU_BLOCK_W)),
        _const_spec((1, c)),
        _const_spec((SUBLANES, c)),
    ]
    args = [u, gw, gb, lam, h0]
    if combine:
        in_specs += [tile, tile]
        args += [h_other, gate]
    out_dtype = BF16 if combine else F32
    return pl.pallas_call(
        functools.partial(_lru_scan_kernel, reverse=reverse, combine=combine),
        grid=(nblk,),
        in_specs=in_specs,
        out_specs=[tile, pl.BlockSpec((SUBLANES, c), lambda i: (0, 0))],
        out_shape=[jax.ShapeDtypeStruct((rows_total, c), out_dtype),
                   jax.ShapeDtypeStruct((SUBLANES, c), F32)],
        scratch_shapes=[
            pltpu.VMEM((rows, c), F32),
            pltpu.VMEM((rows, c), F32),
            pltpu.VMEM((SUBLANES, c), F32),
        ],
        compiler_params=_cparams(("arbitrary",)),
        name="lru_scan_bwd" if reverse else "lru_scan_fwd",
    )(*args)


def _lru_mixer(xc, xl, mc, ml, g, p, need_ctx, tm, tl):
    w_in, conv_w, conv_b, gw, gb, lam = p
    c = w_in.shape[1] // 2
    gate_c, u_c = _inproj_lru(xc, mc, g, w_in, conv_w, conv_b, tm)
    gate_l, u_l = _inproj_lru(xl, ml, g, w_in, conv_w, conv_b, tm)
    h0 = jnp.zeros((SUBLANES, c), F32)
    tl_c = min(tl, xc.size // xc.shape[-1] // SUBLANES)
    hf_c, s_c = _lru_scan(u_c, gw[0], gb[0], lam[0], h0, reverse=False, tl=tl_c)
    hf_l, _ = _lru_scan(u_l, gw[0], gb[0], lam[0], s_c, reverse=False, tl=tl)
    if need_ctx:
        z_c, s_c = _lru_scan(u_c, gw[1], gb[1], lam[1], h0, reverse=True, tl=tl_c,
                             h_other=hf_c, gate=gate_c)
    else:
        z_c = None
        _, s_c = _lru_scan(u_c, gw[1], gb[1], lam[1], h0, reverse=True, tl=tl_c)
    z_l, _ = _lru_scan(u_l, gw[1], gb[1], lam[1], s_c, reverse=True, tl=tl,
                       h_other=hf_l, gate=gate_l)
    return z_c, z_l


def _prep_lru(w_in, conv_w, conv_b, gate_w, gate_b, lam):
    nb = gate_w.shape[2]
    gw = jnp.concatenate([gate_w[:, 0], gate_w[:, 1]], axis=-1).astype(BF16)
    gb = jnp.concatenate([gate_b[:, 0].reshape(2, nb, 1, LRU_BLOCK_W),
                          gate_b[:, 1].reshape(2, nb, 1, LRU_BLOCK_W)], axis=-1)
    return (w_in.astype(BF16), 0.5 * conv_w, 0.5 * conv_b.reshape(1, -1), gw, 0.5 * gb,
            lam.reshape(2, 1, -1))


NA_HEAD_DIM = 64
GRID_W = 64
WIN_ROWS = 8
WIN_COLS = 16
NEG = -1e30
LOG2E = 1.4426950408889634
NA_LOOKAHEAD = 2


def _inproj_na_kernel(x_ref, m_ref, g_ref, w_ref, qg_ref, kg_ref, ones_ref, q_ref, k_ref, v_ref, y_scr):
    d = x_ref.shape[1]
    tl = x_ref.shape[0] // SUBLANES
    h = _norm_mod(x_ref[...], g_ref[...], m_ref[:, 0:d], m_ref[:, d:2 * d])
    y = jnp.dot(h.astype(BF16), w_ref[...], preferred_element_type=F32)
    cw = ones_ref.shape[0]
    for part, gain_ref in ((0, qg_ref), (1, kg_ref)):
        for j in range(d // cw):
            lo = part * d + j * cw
            t = y[:, lo:lo + cw]
            ss = jnp.dot((t * t).astype(BF16), ones_ref[...], preferred_element_type=F32)
            tn = t * lax.rsqrt(ss * (1.0 / NA_HEAD_DIM) + EPS) * gain_ref[:, j * cw:(j + 1) * cw]
            for s in range(cw // LANES):
                y_scr[lo // LANES + s] = tn[:, s * LANES:(s + 1) * LANES]
    for s in range(d // LANES):
        y_scr[2 * d // LANES + s] = y[:, 2 * d + s * LANES:2 * d + (s + 1) * LANES]
    for b in range(SUBLANES):
        for part, o_ref in ((0, q_ref), (1, k_ref), (2, v_ref)):
            for s in range(d // LANES):
                o_ref[b, :, s * LANES:(s + 1) * LANES] = (
                    y_scr[part * (d // LANES) + s, pl.ds(b, tl, stride=SUBLANES), :].astype(BF16))


def _inproj_na(x, m, g, w_qkv, qg, kg, ones_bd, tm):
    rows, d = x.shape
    t_total = rows // SUBLANES
    tl = tm // SUBLANES
    out = jax.ShapeDtypeStruct((SUBLANES, t_total, d), BF16)
    ospec = pl.BlockSpec((SUBLANES, tl, d), lambda i: (0, i, 0))
    return pl.pallas_call(
        _inproj_na_kernel,
        grid=(rows // tm,),
        in_specs=[
            pl.BlockSpec((tm, d), lambda i: (i, 0)),
            _const_spec((SUBLANES, 6 * d)),
            _const_spec((1, d)),
            _const_spec((d, 3 * d)),
            _const_spec((1, d)),
            _const_spec((1, d)),
            _const_spec(ones_bd.shape),
        ],
        out_specs=[ospec, ospec, ospec],
        out_shape=[out, out, out],
        scratch_shapes=[pltpu.VMEM((3 * d // LANES, tm, LANES), F32)],
        compiler_params=_cparams(("arbitrary",)),
        name="inproj_na",
    )(x, m, g, w_qkv, qg, kg, ones_bd)


def _rpb_expand_kernel(rpb_ref, o_ref):
    p = pl.program_id(0)
    n_ro, n_co = rpb_ref.shape[1], rpb_ref.shape[2]
    c = lax.broadcasted_iota(jnp.int32, (GRID_W, 2 * GRID_W), 0)
    lane = lax.broadcasted_iota(jnp.int32, (GRID_W, 2 * GRID_W), 1)
    second = lane >= GRID_W
    q = jnp.where(second, lane - GRID_W, lane)
    cs = jnp.clip(q - WIN_COLS // 2, 0, GRID_W - WIN_COLS)
    valid = (c >= cs) & (c < cs + WIN_COLS)
    off = c - q + (WIN_COLS - 1)
    tiles = []
    for ro in range(n_ro):
        acc = jnp.full((GRID_W, 2 * GRID_W), NEG, F32)
        for j in range(n_co):
            val = jnp.where(second, rpb_ref[2 * p + 1, ro, j] * LOG2E, rpb_ref[2 * p, ro, j] * LOG2E)
            acc = jnp.where(off == j, val, acc)
        tiles.append(jnp.where(valid, acc, NEG))
    for r0 in range(n_ro - WIN_ROWS + 1):
        for kr in range(WIN_ROWS):
            o_ref[r0, 0, kr * GRID_W:(kr + 1) * GRID_W, :] = tiles[r0 + kr]


def _rpb_tables(rpb):
    nh, n_ro, _ = rpb.shape
    ncase = n_ro - WIN_ROWS + 1
    shape = (ncase, nh // 2, WIN_ROWS * GRID_W, 2 * GRID_W)
    return pl.pallas_call(
        _rpb_expand_kernel,
        grid=(nh // 2,),
        in_specs=[pl.BlockSpec(memory_space=pltpu.SMEM)],
        out_specs=pl.BlockSpec((ncase, 1) + shape[2:], lambda p: (0, p, 0, 0)),
        out_shape=jax.ShapeDtypeStruct(shape, F32),
        compiler_params=_cparams(("arbitrary",)),
        name="rpb_expand",
    )(rpb)


def _attn_pair(qp, parts):
    return _attn_finish(_attn_scores(qp, [(k, bias) for k, _, bias in parts]),
                        [v for _, v, _ in parts], qp.shape[0])


def _attn_scores(qp, parts):
    lane = lax.broadcasted_iota(jnp.int32, qp.shape, 1)
    zero = jnp.zeros_like(qp)
    qbd = jnp.concatenate([jnp.where(lane < NA_HEAD_DIM, qp, zero),
                           jnp.where(lane >= NA_HEAD_DIM, qp, zero)], axis=0)
    qbd_t = jnp.transpose(qbd)
    scores = []
    for k, bias in parts:
        s = jnp.dot(k, qbd_t, preferred_element_type=F32)
        scores.append(s if bias is None else s + bias)
    return scores


def _attn_finish(scores, vs, nq):
    mx = functools.reduce(jnp.maximum, [jnp.max(s, axis=0, keepdims=True) for s in scores])
    es = [jnp.exp2(s - mx) for s in scores]
    den = functools.reduce(jnp.add, [jnp.sum(e, axis=0, keepdims=True) for e in es])
    tn = (((0,), (0,)), ((), ()))
    rt = functools.reduce(jnp.add, [
        lax.dot_general(v, e.astype(BF16), tn, preferred_element_type=F32)
        for e, v in zip(es, vs)])
    r = jnp.transpose(rt * (1.0 / den))
    lane_o = lax.broadcasted_iota(jnp.int32, (nq, LANES), 1)
    return jnp.where(lane_o < NA_HEAD_DIM, r[:nq], r[nq:])


def _na_attn_kernel(q_ref, k_ref, v_ref, kc_ref, vc_ref, bias_ref, o_ref, *, rows):
    r = pl.program_id(1)
    rs = jnp.clip(r - WIN_ROWS // 2, 0, rows - WIN_ROWS)
    start = pl.multiple_of(rs * GRID_W, GRID_W)
    nwin = WIN_ROWS * GRID_W
    npairs = q_ref.shape[1] // LANES

    def scores_of(p):
        ls = slice(p * LANES, (p + 1) * LANES)
        return _attn_scores(q_ref[:, ls], [(k_ref[pl.ds(start, nwin), ls], bias_ref[0, p]),
                                           (kc_ref[:, ls], None)])

    ready = [scores_of(p) for p in range(min(NA_LOOKAHEAD, npairs))]
    outs = []
    for p in range(npairs):
        ls = slice(p * LANES, (p + 1) * LANES)
        cur = ready.pop(0)
        if p + NA_LOOKAHEAD < npairs:
            ready.append(scores_of(p + NA_LOOKAHEAD))
        outs.append(_attn_finish(cur, [v_ref[pl.ds(start, nwin), ls], vc_ref[:, ls]], q_ref.shape[0]))
    o_ref[...] = jnp.concatenate(outs, axis=1).astype(o_ref.dtype)


def _na_attn(q, k, v, kc, vc, bias_t):
    b, l, d = q.shape
    ct = kc.shape[1]
    rows = l // GRID_W
    half = WIN_ROWS // 2

    def bias_idx(bi, r):
        rs = jnp.clip(r - half, 0, rows - WIN_ROWS)
        return (rs - r + (WIN_ROWS - 1), 0, 0, 0)

    return pl.pallas_call(
        functools.partial(_na_attn_kernel, rows=rows),
        grid=(b, rows),
        in_specs=[
            pl.BlockSpec((None, GRID_W, d), lambda bi, r: (bi, r, 0)),
            pl.BlockSpec((None, l, d), lambda bi, r: (bi, 0, 0), pipeline_mode=pl.Buffered(1)),
            pl.BlockSpec((None, l, d), lambda bi, r: (bi, 0, 0), pipeline_mode=pl.Buffered(1)),
            pl.BlockSpec((None, ct, d), lambda bi, r: (bi, 0, 0)),
            pl.BlockSpec((None, ct, d), lambda bi, r: (bi, 0, 0)),
            pl.BlockSpec((1,) + bias_t.shape[1:], bias_idx),
        ],
        out_specs=pl.BlockSpec((None, GRID_W, d), lambda bi, r: (bi, r, 0)),
        out_shape=jax.ShapeDtypeStruct((b, l, d), BF16),
        compiler_params=_cparams(("arbitrary", "arbitrary")),
        name="na_attn",
    )(q, k, v, kc, vc, bias_t)


def _ctx_attn_kernel(q_ref, k_ref, v_ref, o_ref):
    for p in range(q_ref.shape[2] // LANES):
        ls = slice(p * LANES, (p + 1) * LANES)
        o = _attn_pair(q_ref[0, :, ls], [(k_ref[0, :, ls], v_ref[0, :, ls], None)])
        o_ref[0, :, ls] = o.astype(o_ref.dtype)


def _ctx_attn(q, k, v):
    b, ct, d = q.shape
    spec = pl.BlockSpec((1, ct, d), lambda bi: (bi, 0, 0))
    return pl.pallas_call(
        _ctx_attn_kernel,
        grid=(b,),
        in_specs=[spec, spec, spec],
        out_specs=spec,
        out_shape=jax.ShapeDtypeStruct((b, ct, d), BF16),
        compiler_params=_cparams(("arbitrary",)),
        name="ctx_attn",
    )(q, k, v)


def _prep_na(w_qkv, qg, kg):
    d = w_qkv.shape[0]
    nh = d // NA_HEAD_DIM
    head = jnp.arange(2 * LANES) // NA_HEAD_DIM
    ones_bd = (head[:, None] == head[None, :]).astype(BF16)
    qg_t = jnp.tile(qg, nh).reshape(1, d) * (NA_HEAD_DIM ** -0.5 * LOG2E)
    kg_t = jnp.tile(kg, nh).reshape(1, d)
    return w_qkv.astype(BF16), qg_t, kg_t, ones_bd


def _na_mixer(xc, xl, mc, ml, g, p, bias_t, need_ctx, tm):
    w_qkv, qg_t, kg_t, ones_bd = p
    q_c, k_c, v_c = _inproj_na(xc, mc, g, w_qkv, qg_t, kg_t, ones_bd, tm)
    q_l, k_l, v_l = _inproj_na(xl, ml, g, w_qkv, qg_t, kg_t, ones_bd, tm)
    o_l = _na_attn(q_l, k_l, v_l, k_c, v_c, bias_t)
    o_c = _ctx_attn(q_c, k_c, v_c) if need_ctx else None
    return o_c, o_l


HG_DK = 128
HG_CHUNK = 64
HG_SUB = 16
HG_HEADS_PER_STEP = 2
HG_SAFE_LOG2 = 100.0


def _lb_kernel(x_ref, o_ref):
    x = x_ref[...]
    e = jnp.exp(x - jnp.max(x, axis=0, keepdims=True))
    p = e / jnp.sum(e, axis=0, keepdims=True)
    acc = jnp.zeros_like(p[0:1])
    for l in range(x.shape[0]):
        o_ref[l:l + 1, :] = acc
        if l + 1 < x.shape[0]:
            acc = acc + p[l + 1:l + 2]


def _lower_bounds(lb_logits):
    return pl.pallas_call(
        _lb_kernel,
        out_shape=jax.ShapeDtypeStruct(lb_logits.shape, F32),
        name="hg_lower_bounds",
    )(lb_logits)


def _inproj_hg_kernel(x_ref, m_ref, g_ref, w_ref, q_ref, v_ref, gs_ref, zf_ref, zb_ref):
    d = x_ref.shape[1]
    h = _norm_mod(x_ref[...], g_ref[...], m_ref[:, 0:d], m_ref[:, d:2 * d])
    y = jnp.dot(h.astype(BF16), w_ref[...], preferred_element_type=F32)
    q_ref[...] = _silu(y[:, 0:d]).astype(BF16)
    v_ref[...] = y[:, d:2 * d].astype(BF16)
    gs_ref[...] = _silu(y[:, 2 * d:3 * d]).astype(BF16)
    zf_ref[...] = y[:, 3 * d:4 * d]
    zb_ref[...] = y[:, 4 * d:5 * d]


def _inproj_hg(x, m, g, w_in, tm):
    rows, d = x.shape
    spec = pl.BlockSpec((tm, d), lambda i: (i, 0))
    return pl.pallas_call(
        _inproj_hg_kernel,
        grid=(rows // tm,),
        in_specs=[spec, _const_spec((SUBLANES, 6 * d)), _const_spec((1, d)), _const_spec((d, 5 * d))],
        out_specs=[spec] * 5,
        out_shape=[jax.ShapeDtypeStruct((rows, d), BF16)] * 3 + [jax.ShapeDtypeStruct((rows, d), F32)] * 2,
        compiler_params=_cparams(("arbitrary",)),
        name="inproj_hg",
    )(x, m, g, w_in)


def _hg_scan_kernel(*refs, reverse, combine):
    nin = 8 if combine else 5
    if combine:
        q_ref, v_ref, z_ref, lb_ref, s0_ref, oo_ref, gs_ref, ng_ref = refs[:nin]
    else:
        q_ref, v_ref, z_ref, lb_ref, s0_ref = refs[:nin]
        oo_ref = gs_ref = ng_ref = None
    out_ref, sout_ref, s_scr = refs[nin:nin + 3]
    hp = q_ref.shape[1] // LANES
    nwork = 7
    work = [refs[nin + 3 + nwork * hh:nin + 3 + nwork * (hh + 1)] for hh in range(hp)]
    h = pl.program_id(0)
    i = pl.program_id(1)

    @pl.when(i == 0)
    def _():
        s_scr[...] = s0_ref[...]

    @pl.when((i == 0) & (h == 0))
    def _():
        for qe_scr, ke_scr, *_ in work:
            qe_scr[...] = jnp.zeros_like(qe_scr)
            ke_scr[...] = jnp.zeros_like(ke_scr)

    for hh in range(hp):
        def cols(ref):
            return None if ref is None else ref.at[:, hh * LANES:(hh + 1) * LANES]
        _hg_head(cols(q_ref), cols(v_ref), cols(z_ref), cols(lb_ref), cols(oo_ref), cols(gs_ref), ng_ref,
                 cols(out_ref), sout_ref.at[hh], s_scr.at[hh], *work[hh], reverse, combine)


def _hg_head(q_ref, v_ref, z_ref, lb_ref, oo_ref, gs_ref, ng_ref, out_ref, sout_ref, s_scr, qe_scr,
             ke_scr, qt_scr, kt_scr, kh_scr, v_scr, o_scr, reverse, combine):
    nsteps = HG_CHUNK
    nsub = nsteps // HG_SUB
    rows = nsteps * SUBLANES
    order = range(nsteps - 1, -1, -1) if reverse else range(nsteps)

    def load_steps(ref):
        x3 = ref[...].astype(F32).reshape(nsteps, SUBLANES, LANES)
        return jnp.stack([x3[t] for t in order]) if reverse else x3

    def store_steps(val3):
        return jnp.concatenate([val3[s] for s in order], axis=0)

    lb = lb_ref[...]
    q = load_steps(q_ref)
    v = load_steps(v_ref)
    sg = _sigmoid(load_steps(z_ref))
    k = (1.0 - lb) * (1.0 - sg)
    logf = jnp.log(lb + (1.0 - lb) * sg) * LOG2E
    acc = logf[0]
    cum = [acc]
    for s in range(1, nsteps):
        acc = acc + logf[s]
        cum.append(acc)
    b = jnp.stack(cum)
    b_end = cum[-1]

    v_scr[...] = v.reshape(rows, LANES)
    qt_scr[...] = (q * jnp.exp2(b)).reshape(rows, LANES)
    kt_scr[...] = (k * jnp.exp2(b_end[None] - b)).reshape(rows, LANES)

    nt = (((1,), (1,)), ((), ()))
    tn = (((0,), (0,)), ((), ()))
    decay_end = jnp.exp2(b_end)

    def batch_updates(a_intra):
        results = []
        for bb in range(SUBLANES):
            sl = pl.ds(bb, nsteps, stride=SUBLANES)
            vb = v_scr[sl, :].astype(BF16)
            st = s_scr[bb]
            o_b = (jnp.dot(a_intra[bb].astype(BF16), vb, preferred_element_type=F32)
                   + lax.dot_general(qt_scr[sl, :].astype(BF16), st.astype(BF16), nt,
                                     preferred_element_type=F32))
            s_new = st * decay_end[bb:bb + 1, :] + lax.dot_general(
                vb, kt_scr[sl, :].astype(BF16), tn, preferred_element_type=F32)
            results.append((o_b, s_new))
        for bb, (o_b, s_new) in enumerate(results):
            o_scr[pl.ds(bb, nsteps, stride=SUBLANES), :] = o_b
            s_scr[bb] = s_new
            sout_ref[bb] = s_new

    safe = jnp.min(b_end) > -HG_SAFE_LOG2

    @pl.when(safe)
    def _():
        kh_scr[...] = (k * jnp.exp2(-b)).reshape(rows, LANES)
        t_idx = lax.broadcasted_iota(jnp.int32, (nsteps, nsteps), 0)
        s_idx = lax.broadcasted_iota(jnp.int32, (nsteps, nsteps), 1)
        a_all = []
        for bb in range(SUBLANES):
            sl = pl.ds(bb, nsteps, stride=SUBLANES)
            a = lax.dot_general(qt_scr[sl, :].astype(BF16), kh_scr[sl, :].astype(BF16), nt,
                                preferred_element_type=F32)
            a_all.append(jnp.where(s_idx <= t_idx, a, 0.0))
        batch_updates(a_all)

    @pl.when(jnp.logical_not(safe))
    def _():
        for blk in range(1, nsub):
            lo = blk * HG_SUB
            beta = cum[lo - 1][None]
            qe_scr[blk - 1, lo * SUBLANES:(lo + HG_SUB) * SUBLANES, :] = (
                q[lo:lo + HG_SUB] * jnp.exp2(b[lo:lo + HG_SUB] - beta)).reshape(HG_SUB * SUBLANES, LANES)
            ke_scr[blk - 1, 0:lo * SUBLANES, :] = (
                k[:lo] * jnp.exp2(beta - b[:lo])).reshape(lo * SUBLANES, LANES)
        a_all = []
        for bb in range(SUBLANES):
            sl = pl.ds(bb, nsteps, stride=SUBLANES)
            qe = jnp.concatenate([qe_scr[j, sl, :] for j in range(nsub - 1)], axis=1).astype(BF16)
            ke = jnp.concatenate([ke_scr[j, sl, :] for j in range(nsub - 1)], axis=1).astype(BF16)
            a_all.append(lax.dot_general(qe, ke, nt, preferred_element_type=F32))
        batch_updates(a_all)
        shp = (nsub, HG_SUB, SUBLANES, LANES)
        b4, q4, k4, v4 = b.reshape(shp), q.reshape(shp), k.reshape(shp), v.reshape(shp)
        near0 = jnp.sum(q4 * k4, axis=-1, keepdims=True) * v4
        near = [near0[:, t] for t in range(HG_SUB)]
        for dlt in range(1, HG_SUB):
            n = HG_SUB - dlt
            w = q4[:, dlt:] * k4[:, :n] * jnp.exp2(b4[:, dlt:] - b4[:, :n])
            contrib = jnp.sum(w, axis=-1, keepdims=True) * v4[:, :n]
            for j in range(n):
                near[dlt + j] = near[dlt + j] + contrib[:, j]
        o_scr[...] = o_scr[...] + jnp.stack(near, axis=1).reshape(rows, LANES)

    o_tot = store_steps(o_scr[...].reshape(nsteps, SUBLANES, LANES))
    if combine:
        tot = oo_ref[...] + o_tot
        y = tot * lax.rsqrt(jnp.mean(tot * tot, axis=-1, keepdims=True) + EPS) * ng_ref[...]
        out_ref[...] = (y * gs_ref[...].astype(F32)).astype(out_ref.dtype)
    else:
        out_ref[...] = o_tot


def _hg_scan(q, v, z, lb, s0, *, reverse, o_other=None, gs=None, norm_g=None):
    rows_total, d = q.shape
    nh = d // HG_DK
    rows = HG_CHUNK * SUBLANES
    nblk = rows_total // rows
    combine = o_other is not None

    def blk(i):
        return (nblk - 1 - i) if reverse else i

    hp = HG_HEADS_PER_STEP
    tile = pl.BlockSpec((rows, hp * HG_DK), lambda h, i: (blk(i), h))
    state = pl.BlockSpec((hp, SUBLANES, HG_DK, HG_DK), lambda h, i: (h, 0, 0, 0))
    in_specs = [tile, tile, tile, pl.BlockSpec((1, hp * HG_DK), lambda h, i: (0, h)), state]
    args = [q, v, z, lb, s0]
    if combine:
        in_specs += [tile, tile, pl.BlockSpec((1, HG_DK), lambda h, i: (0, 0))]
        args += [o_other, gs, norm_g]
    nref = HG_CHUNK // HG_SUB - 1
    return pl.pallas_call(
        functools.partial(_hg_scan_kernel, reverse=reverse, combine=combine),
        grid=(nh // hp, nblk),
        in_specs=in_specs,
        out_specs=[tile, state],
        out_shape=[jax.ShapeDtypeStruct((rows_total, d), BF16 if combine else F32),
                   jax.ShapeDtypeStruct(s0.shape, F32)],
        scratch_shapes=[pltpu.VMEM((hp, SUBLANES, HG_DK, HG_DK), F32)] + hp * [
            pltpu.VMEM((nref, rows, LANES), F32),
            pltpu.VMEM((nref, rows, LANES), F32),
            pltpu.VMEM((rows, LANES), F32),
            pltpu.VMEM((rows, LANES), F32),
            pltpu.VMEM((rows, LANES), F32),
            pltpu.VMEM((rows, LANES), F32),
            pltpu.VMEM((rows, LANES), F32),
        ],
        compiler_params=_cparams(("arbitrary", "arbitrary")),
        name="hg_scan_bwd" if reverse else "hg_scan_fwd",
    )(*args)


def _hg_mixer(xc, xl, mc, ml, g, w_in, lb, norm_g, need_ctx, tm):
    d = xl.shape[1]
    nh = d // HG_DK
    q_c, v_c, gs_c, zf_c, zb_c = _inproj_hg(xc, mc, g, w_in, tm)
    q_l, v_l, gs_l, zf_l, zb_l = _inproj_hg(xl, ml, g, w_in, tm)
    s0 = jnp.zeros((nh, SUBLANES, HG_DK, HG_DK), F32)
    of_c, s_c = _hg_scan(q_c, v_c, zf_c, lb, s0, reverse=False)
    of_l, _ = _hg_scan(q_l, v_l, zf_l, lb, s_c, reverse=False)
    if need_ctx:
        z_c, s_c = _hg_scan(q_c, v_c, zb_c, lb, s0, reverse=True, o_other=of_c, gs=gs_c, norm_g=norm_g)
    else:
        z_c = None
        _, s_c = _hg_scan(q_c, v_c, zb_c, lb, s0, reverse=True)
    z_l, _ = _hg_scan(q_l, v_l, zb_l, lb, s_c, reverse=True, o_other=of_l, gs=gs_l, norm_g=norm_g)
    return z_c, z_l


def kernel(x, c, ctx, c_ctx, mod_w, mod_b, norm_mix_g, norm_ffn_g, ffn_w_gu, ffn_w_down, lru_w_in, lru_conv_w, lru_conv_b, lru_gate_w, lru_gate_b, lru_lambda, lru_w_out, na_w_qkv, na_q_norm_g, na_k_norm_g, na_rpb, na_w_o, hg_w_in, hg_lb_logits, hg_norm_g, hg_w_o):
    b, l, d = x.shape
    ct = ctx.shape[1]
    depth = mod_w.shape[0]
    tm = ROW_TILE
    cc = jnp.zeros((2 * SUBLANES, d), F32).at[:b].set(c).at[b].set(c_ctx)
    mod = _modulation(cc, mod_w, mod_b)
    lower_bounds = _lower_bounds(hg_lb_logits)
    xl, xc = x, ctx
    for layer in range(depth):
        kind, slot = layer % 3, layer // 3
        need_ctx = layer < depth - 1
        ml = mod[layer, :b]
        mc = jnp.broadcast_to(mod[layer, b:b + 1], (b, 6 * d))
        g = norm_mix_g[layer].reshape(1, d)
        gf = norm_ffn_g[layer].reshape(1, d)
        wgu = ffn_w_gu[layer].astype(BF16)
        wd = ffn_w_down[layer].astype(BF16)
        if kind == 0:
            p = _prep_lru(lru_w_in[slot], lru_conv_w[slot], lru_conv_b[slot], lru_gate_w[slot],
                          lru_gate_b[slot], lru_lambda[slot])
            zc, zl = _lru_mixer(xc, xl, mc, ml, g, p, need_ctx, tm, LRU_STEPS)
            wo = lru_w_out[slot].astype(BF16)
        elif kind == 1:
            p = _prep_na(na_w_qkv[slot], na_q_norm_g[slot], na_k_norm_g[slot])
            zc, zl = _na_mixer(xc, xl, mc, ml, g, p, _rpb_tables(na_rpb[slot]), need_ctx, tm)
            wo = na_w_o[slot].astype(BF16)
        else:
            zc, zl = _hg_mixer(xc, xl, mc, ml, g, hg_w_in[slot].astype(BF16),
                               lower_bounds[layer:layer + 1], hg_norm_g[slot].reshape(1, -1), need_ctx, tm)
            wo = hg_w_o[slot].astype(BF16)
        xl = _out_ffn(xl, zl, ml, gf, wo, wgu, wd, tm, out_per_batch=layer == depth - 1)
        if need_ctx:
            xc = _out_ffn(xc, zc, mc, gf, wo, wgu, wd, tm)
    return xl
```

```python
import functools

import jax
import jax.numpy as jnp
from jax import lax
from jax.experimental import pallas as pl
from jax.experimental.pallas import tpu as pltpu

F32 = jnp.float32
BF16 = jnp.bfloat16

SUBLANES = 8
LANES = 128
EPS = 1e-6
LRU_C = 8.0
LRU_BLOCK_W = 128
CONV_W = 4
CONV_LEFT = 2
VMEM_LIMIT = 56 * 1024 * 1024
MXU_TILE = 256
FFN_CHUNKS = 2
ROW_TILE = 512
LRU_STEPS = 64


def _split_aligned(n, parts, align):
    if n % align:
        return [(0, n)]
    tiles = n // align
    cuts = [round(i * tiles / parts) * align for i in range(parts + 1)]
    return [(lo, hi) for lo, hi in zip(cuts[:-1], cuts[1:]) if hi > lo]


def _cparams(sem):
    return pltpu.CompilerParams(dimension_semantics=sem, vmem_limit_bytes=VMEM_LIMIT)


def _const_spec(shape):
    nd = len(shape)
    return pl.BlockSpec(shape, lambda *_: (0,) * nd, pipeline_mode=pl.Buffered(1))


def _sigmoid(x):
    return 0.5 * jnp.tanh(0.5 * x) + 0.5


def _silu(x):
    return x * _sigmoid(x)


def _gelu_tanh(x):
    return 0.5 * x * (1.0 + jnp.tanh(0.7978845608028654 * (x + 0.044715 * (x * x * x))))


def _t3(x):
    return x.reshape(x.shape[0] // SUBLANES, SUBLANES, x.shape[1])


def _norm_mod(x, g, shift, scale):
    ms = jnp.mean(x * x, axis=-1, keepdims=True)
    y = x * lax.rsqrt(ms + EPS) * g
    y3 = _t3(y) * (1.0 + scale)[None] + shift[None]
    return y3.reshape(x.shape)


def _mod_kernel(c_ref, w_ref, b_ref, o_ref):
    a = _silu(c_ref[...])
    o_ref[0] = jnp.dot(a, w_ref[0], preferred_element_type=F32) + b_ref[0]


def _modulation(cc, mod_w, mod_b):
    depth, d, d6 = mod_w.shape
    nblk = d6 // d
    return pl.pallas_call(
        _mod_kernel,
        grid=(depth, nblk),
        in_specs=[
            pl.BlockSpec((2 * SUBLANES, d), lambda l, j: (0, 0)),
            pl.BlockSpec((1, d, d), lambda l, j: (l, 0, j)),
            pl.BlockSpec((1, 1, d), lambda l, j: (l, 0, j)),
        ],
        out_specs=pl.BlockSpec((1, 2 * SUBLANES, d), lambda l, j: (l, 0, j)),
        out_shape=jax.ShapeDtypeStruct((depth, 2 * SUBLANES, d6), F32),
        compiler_params=_cparams(("arbitrary", "arbitrary")),
        name="modulation",
    )(cc, mod_w, mod_b.reshape(depth, 1, d6))


def _rows_from_batches(ref3, scr):
    tl = ref3.shape[1]
    nslab = ref3.shape[2] // LANES
    for b in range(SUBLANES):
        for s in range(nslab):
            scr[s, pl.ds(b, tl, stride=SUBLANES), :] = ref3[b, :, s * LANES:(s + 1) * LANES].astype(F32)
    return jnp.concatenate([scr[s] for s in range(nslab)], axis=1)


def _rows_to_batches(val, ref3, scr):
    tl = ref3.shape[1]
    nslab = ref3.shape[2] // LANES
    for s in range(nslab):
        scr[s] = val[:, s * LANES:(s + 1) * LANES]
    for b in range(SUBLANES):
        for s in range(nslab):
            ref3[b, :, s * LANES:(s + 1) * LANES] = scr[s, pl.ds(b, tl, stride=SUBLANES), :].astype(ref3.dtype)


def _slab_scratch(tm, cols):
    return pltpu.VMEM((cols // LANES, tm, LANES), F32)


def _out_ffn_kernel(x_ref, z_ref, m_ref, gf_ref, wo_ref, wgu_ref, wd_ref, o_ref, *scratch):
    d = m_ref.shape[1] // 6
    f = wd_ref.shape[0]
    scratch = list(scratch)
    x = _rows_from_batches(x_ref, scratch.pop(0)) if len(x_ref.shape) == 3 else x_ref[...]
    z = _rows_from_batches(z_ref, scratch.pop(0)).astype(BF16) if len(z_ref.shape) == 3 else z_ref[...]
    y = jnp.dot(z, wo_ref[...], preferred_element_type=F32)
    x1 = (_t3(x) + m_ref[:, 2 * d:3 * d][None] * _t3(y)).reshape(x.shape)
    h = _norm_mod(x1, gf_ref[...], m_ref[:, 3 * d:4 * d], m_ref[:, 4 * d:5 * d])
    hb = h.astype(BF16)
    y2 = None
    for lo, hi in _split_aligned(f, FFN_CHUNKS, MXU_TILE):
        a = jnp.dot(hb, wgu_ref[:, lo:hi], preferred_element_type=F32)
        g = jnp.dot(hb, wgu_ref[:, f + lo:f + hi], preferred_element_type=F32)
        part = jnp.dot((_silu(a) * g).astype(BF16), wd_ref[lo:hi, :], preferred_element_type=F32)
        y2 = part if y2 is None else y2 + part
    out = (_t3(x1) + m_ref[:, 5 * d:6 * d][None] * _t3(y2)).reshape(x.shape)
    if len(o_ref.shape) == 3:
        _rows_to_batches(out, o_ref, scratch.pop(0))
    else:
        o_ref[...] = out


def _row_spec(a, tm):
    if a.ndim == 3:
        return pl.BlockSpec((SUBLANES, tm // SUBLANES, a.shape[2]), lambda i: (0, i, 0))
    return pl.BlockSpec((tm, a.shape[1]), lambda i: (i, 0))


def _out_ffn(x, z, m, gf, wo, wgu, wd, tm, out_per_batch=False):
    d = x.shape[-1]
    rows = x.size // d
    kz = z.shape[-1]
    f = wd.shape[0]
    scratch = [_slab_scratch(tm, a.shape[-1]) for a in (x, z) if a.ndim == 3]
    if out_per_batch:
        scratch.append(_slab_scratch(tm, d))
        out_shape = jax.ShapeDtypeStruct((SUBLANES, rows // SUBLANES, d), F32)
    else:
        out_shape = jax.ShapeDtypeStruct((rows, d), F32)
    return pl.pallas_call(
        _out_ffn_kernel,
        grid=(rows // tm,),
        scratch_shapes=scratch,
        in_specs=[
            _row_spec(x, tm),
            _row_spec(z, tm),
            _const_spec((SUBLANES, 6 * d)),
            _const_spec((1, d)),
            _const_spec((kz, d)),
            _const_spec((d, 2 * f)),
            _const_spec((f, d)),
        ],
        out_specs=_row_spec(out_shape, tm),
        out_shape=out_shape,
        compiler_params=_cparams(("arbitrary",)),
        name="out_ffn",
    )(x, z, m, gf, wo, wgu, wd)


def _inproj_lru_kernel(x_ref, xp_ref, xn_ref, m_ref, g_ref, w_ref, cw_ref, cb_ref, gate_ref, u_ref,
                       *scratch, nblk):
    i = pl.program_id(0)
    d = m_ref.shape[1] // 6
    rows, c = u_ref.shape
    halo = LRU_HALO
    if scratch:
        xs = [_rows_from_batches(xp_ref, scratch[0])[-halo:], _rows_from_batches(x_ref, scratch[1]),
              _rows_from_batches(xn_ref, scratch[2])[:halo]]
    else:
        xs = [xp_ref[...], x_ref[...], xn_ref[...]]
    shift, scale = m_ref[:, 0:d], m_ref[:, d:2 * d]
    h = jnp.concatenate([_norm_mod(x, g_ref[...], shift, scale).astype(BF16) for x in xs],
                        axis=0)
    y = jnp.dot(h, w_ref[...], preferred_element_type=F32)
    gate_ref[...] = _gelu_tanh(y[halo:halo + rows, :c]).astype(BF16)
    u = y[:, c:]
    row = lax.broadcasted_iota(jnp.int32, (u.shape[0], 1), 0)
    inside = ((row >= halo) | (i > 0)) & ((row < halo + rows) | (i < nblk - 1))
    u = jnp.where(inside, u, 0.0)
    first = halo - CONV_LEFT * SUBLANES
    acc = cb_ref[...] + cw_ref[0:1, :] * u[first:first + rows]
    for k in range(1, CONV_W):
        acc = acc + cw_ref[k:k + 1, :] * u[first + k * SUBLANES:first + k * SUBLANES + rows]
    u_ref[...] = acc.astype(u_ref.dtype)


LRU_STORE = BF16
LRU_HALO = 16


def _inproj_lru(x, m, g, w_in, conv_w, conv_b, tm):
    d = x.shape[-1]
    rows = x.size // d
    c = w_in.shape[1] // 2
    nblk = rows // tm
    if x.ndim == 3:
        per = tm // (SUBLANES * SUBLANES)
        last = rows // (SUBLANES * SUBLANES) - 1
        halo_specs = [
            pl.BlockSpec((SUBLANES, SUBLANES, d), lambda i: (0, jnp.maximum(i * per - 1, 0), 0)),
            pl.BlockSpec((SUBLANES, SUBLANES, d), lambda i: (0, jnp.minimum((i + 1) * per, last), 0))]
        scratch = [_slab_scratch(SUBLANES * SUBLANES, d), _slab_scratch(tm, d),
                   _slab_scratch(SUBLANES * SUBLANES, d)]
    else:
        per = tm // LRU_HALO
        last = rows // LRU_HALO - 1
        halo_specs = [
            pl.BlockSpec((LRU_HALO, d), lambda i: (jnp.maximum(i * per - 1, 0), 0)),
            pl.BlockSpec((LRU_HALO, d), lambda i: (jnp.minimum((i + 1) * per, last), 0))]
        scratch = []
    return pl.pallas_call(
        functools.partial(_inproj_lru_kernel, nblk=nblk),
        grid=(nblk,),
        scratch_shapes=scratch,
        in_specs=[
            _row_spec(x, tm),
            *halo_specs,
            _const_spec((SUBLANES, 6 * d)),
            _const_spec((1, d)),
            _const_spec((d, 2 * c)),
            _const_spec((CONV_W, c)),
            _const_spec((1, c)),
        ],
        out_specs=[pl.BlockSpec((tm, c), lambda i: (i, 0)),
                   pl.BlockSpec((tm, c), lambda i: (i, 0))],
        out_shape=[jax.ShapeDtypeStruct((rows, c), BF16),
                   jax.ShapeDtypeStruct((rows, c), LRU_STORE)],
        compiler_params=_cparams(("arbitrary",)),
        name="inproj_lru",
    )(x, x, x, m, g, w_in, conv_w, conv_b)


def _lru_scan_kernel(*refs, reverse, combine):
    if combine:
        (u_ref, gw_ref, gb_ref, lam_ref, h0_ref, hother_ref, gate_ref,
         out_ref, hlast_ref, a_scr, b_scr, h_scr) = refs
    else:
        u_ref, gw_ref, gb_ref, lam_ref, h0_ref, out_ref, hlast_ref, a_scr, b_scr, h_scr = refs
    i = pl.program_id(0)
    rows, c = u_ref.shape
    steps = rows // SUBLANES
    nblocks = c // LRU_BLOCK_W

    @pl.when(i == 0)
    def _():
        h_scr[...] = h0_ref[...]

    for n in range(nblocks):
        ls = slice(n * LRU_BLOCK_W, (n + 1) * LRU_BLOCK_W)
        u = u_ref[:, ls].astype(F32)
        z = jnp.dot(u.astype(BF16), gw_ref[n], preferred_element_type=F32) + gb_ref[n]
        tr = jnp.tanh(z[:, :LRU_BLOCK_W])
        ti = jnp.tanh(z[:, LRU_BLOCK_W:])
        lam = lam_ref[:, ls]
        softplus_neg = jnp.maximum(-lam, 0.0) + jnp.log1p(jnp.exp(-jnp.abs(lam)))
        half_c = (-0.5 * LRU_C) * softplus_neg
        log_a = half_c * tr + half_c
        a = jnp.exp(log_a)
        one_m_a2 = jnp.tanh(log_a) * (-1.0 - a * a)
        root = jnp.where(one_m_a2 > 0.0, one_m_a2 * lax.rsqrt(one_m_a2), 0.0)
        a_scr[:, ls] = a
        b_scr[:, ls] = root * ((ti + 1.0) * u)

    def step(s, h):
        t = (steps - 1 - s) if reverse else s
        off = pl.multiple_of(t * SUBLANES, SUBLANES)
        h = a_scr[pl.ds(off, SUBLANES), :] * h + b_scr[pl.ds(off, SUBLANES), :]
        b_scr[pl.ds(off, SUBLANES), :] = h
        return h

    h_fin = lax.fori_loop(0, steps, step, h_scr[...])
    h_scr[...] = h_fin
    hlast_ref[...] = h_fin
    if combine:
        rec = hother_ref[...].astype(F32) + b_scr[...]
        out_ref[...] = (rec * gate_ref[...].astype(F32)).astype(out_ref.dtype)
    else:
        out_ref[...] = b_scr[...].astype(out_ref.dtype)


def _lru_scan(u, gw, gb, lam, h0, *, reverse, tl, h_other=None, gate=None):
    rows_total, c = u.shape
    rows = tl * SUBLANES
    nblk = rows_total // rows
    combine = h_other is not None
    nblocks = c // LRU_BLOCK_W

    def blk(i):
        return (nblk - 1 - i) if reverse else i

    tile = pl.BlockSpec((rows, c), lambda i: (blk(i), 0))
    in_specs = [
        tile,
        _const_spec((nblocks, LRU_BLOCK_W, 2 * LRU_BLOCK_W)),
        _const_spec((nblocks, 1, 2 * LRU_BLOCK_W)),
        _const_spec((1, c)),
        _const_spec((SUBLANES, c)),
    ]
    args = [u, gw, gb, lam, h0]
    if combine:
        in_specs += [tile, tile]
        args += [h_other, gate]
    out_dtype = BF16 if combine else LRU_STORE
    return pl.pallas_call(
        functools.partial(_lru_scan_kernel, reverse=reverse, combine=combine),
        grid=(nblk,),
        in_specs=in_specs,
        out_specs=[tile, pl.BlockSpec((SUBLANES, c), lambda i: (0, 0))],
        out_shape=[jax.ShapeDtypeStruct((rows_total, c), out_dtype),
                   jax.ShapeDtypeStruct((SUBLANES, c), F32)],
        scratch_shapes=[
            pltpu.VMEM((rows, c), F32),
            pltpu.VMEM((rows, c), F32),
            pltpu.VMEM((SUBLANES, c), F32),
        ],
        compiler_params=_cparams(("arbitrary",)),
        name="lru_scan_bwd" if reverse else "lru_scan_fwd",
    )(*args)


def _lru_mixer(xc, xl, mc, ml, g, p, need_ctx, tm, tl):
    w_in, conv_w, conv_b, gw, gb, lam = p
    c = w_in.shape[1] // 2
    gate_c, u_c = _inproj_lru(xc, mc, g, w_in, conv_w, conv_b, tm)
    gate_l, u_l = _inproj_lru(xl, ml, g, w_in, conv_w, conv_b, tm)
    h0 = jnp.zeros((SUBLANES, c), F32)
    tl_c = min(tl, xc.size // xc.shape[-1] // SUBLANES)
    hf_c, s_c = _lru_scan(u_c, gw[0], gb[0], lam[0], h0, reverse=False, tl=tl_c)
    hf_l, _ = _lru_scan(u_l, gw[0], gb[0], lam[0], s_c, reverse=False, tl=tl)
    if need_ctx:
        z_c, s_c = _lru_scan(u_c, gw[1], gb[1], lam[1], h0, reverse=True, tl=tl_c,
                             h_other=hf_c, gate=gate_c)
    else:
        z_c = None
        _, s_c = _lru_scan(u_c, gw[1], gb[1], lam[1], h0, reverse=True, tl=tl_c)
    z_l, _ = _lru_scan(u_l, gw[1], gb[1], lam[1], s_c, reverse=True, tl=tl,
                       h_other=hf_l, gate=gate_l)
    return z_c, z_l


def _prep_lru(w_in, conv_w, conv_b, gate_w, gate_b, lam):
    nb = gate_w.shape[2]
    gw = jnp.concatenate([gate_w[:, 0], gate_w[:, 1]], axis=-1).astype(BF16)
    gb = jnp.concatenate([gate_b[:, 0].reshape(2, nb, 1, LRU_BLOCK_W),
                          gate_b[:, 1].reshape(2, nb, 1, LRU_BLOCK_W)], axis=-1)
    return (w_in.astype(BF16), 0.5 * conv_w, 0.5 * conv_b.reshape(1, -1), gw, 0.5 * gb,
            lam.reshape(2, 1, -1))


NA_HEAD_DIM = 64
GRID_W = 64
WIN_ROWS = 8
WIN_COLS = 16
NEG = -1e30
LOG2E = 1.4426950408889634
NA_LOOKAHEAD = 2


def _inproj_na_kernel(x_ref, m_ref, g_ref, w_ref, qg_ref, kg_ref, ones_ref, q_ref, k_ref, v_ref, y_scr):
    d = x_ref.shape[1]
    tl = x_ref.shape[0] // SUBLANES
    h = _norm_mod(x_ref[...], g_ref[...], m_ref[:, 0:d], m_ref[:, d:2 * d])
    y = jnp.dot(h.astype(BF16), w_ref[...], preferred_element_type=F32)
    cw = ones_ref.shape[0]
    for part, gain_ref in ((0, qg_ref), (1, kg_ref)):
        for j in range(d // cw):
            lo = part * d + j * cw
            t = y[:, lo:lo + cw]
            ss = jnp.dot((t * t).astype(BF16), ones_ref[...], preferred_element_type=F32)
            tn = t * lax.rsqrt(ss * (1.0 / NA_HEAD_DIM) + EPS) * gain_ref[:, j * cw:(j + 1) * cw]
            for s in range(cw // LANES):
                y_scr[lo // LANES + s] = tn[:, s * LANES:(s + 1) * LANES]
    for s in range(d // LANES):
        y_scr[2 * d // LANES + s] = y[:, 2 * d + s * LANES:2 * d + (s + 1) * LANES]
    for b in range(SUBLANES):
        for part, o_ref in ((0, q_ref), (1, k_ref), (2, v_ref)):
            for s in range(d // LANES):
                o_ref[b, :, s * LANES:(s + 1) * LANES] = _pack_rows(
                    y_scr[part * (d // LANES) + s, pl.ds(b, tl, stride=SUBLANES), :])


def _pack_rows(x):
    return pltpu.bitcast(x.astype(BF16), jnp.int32)


def _unpack_rows(w):
    return pltpu.bitcast(w, BF16)


def _inproj_na(x, m, g, w_qkv, qg, kg, ones_bd, tm):
    rows, d = x.shape
    t_total = rows // SUBLANES
    tl = tm // SUBLANES
    out = jax.ShapeDtypeStruct((SUBLANES, t_total // 2, d), jnp.int32)
    ospec = pl.BlockSpec((SUBLANES, tl // 2, d), lambda i: (0, i, 0))
    return pl.pallas_call(
        _inproj_na_kernel,
        grid=(rows // tm,),
        in_specs=[
            pl.BlockSpec((tm, d), lambda i: (i, 0)),
            _const_spec((SUBLANES, 6 * d)),
            _const_spec((1, d)),
            _const_spec((d, 3 * d)),
            _const_spec((1, d)),
            _const_spec((1, d)),
            _const_spec(ones_bd.shape),
        ],
        out_specs=[ospec, ospec, ospec],
        out_shape=[out, out, out],
        scratch_shapes=[pltpu.VMEM((3 * d // LANES, tm, LANES), F32)],
        compiler_params=_cparams(("arbitrary",)),
        name="inproj_na",
    )(x, m, g, w_qkv, qg, kg, ones_bd)


def _rpb_expand_kernel(rpb_ref, o_ref):
    p = pl.program_id(0)
    n_ro, n_co = rpb_ref.shape[1], rpb_ref.shape[2]
    c = lax.broadcasted_iota(jnp.int32, (GRID_W, 2 * GRID_W), 0)
    lane = lax.broadcasted_iota(jnp.int32, (GRID_W, 2 * GRID_W), 1)
    second = lane >= GRID_W
    q = jnp.where(second, lane - GRID_W, lane)
    cs = jnp.clip(q - WIN_COLS // 2, 0, GRID_W - WIN_COLS)
    valid = (c >= cs) & (c < cs + WIN_COLS)
    off = c - q + (WIN_COLS - 1)
    tiles = []
    for ro in range(n_ro):
        acc = jnp.full((GRID_W, 2 * GRID_W), NEG, F32)
        for j in range(n_co):
            val = jnp.where(second, rpb_ref[2 * p + 1, ro, j] * LOG2E, rpb_ref[2 * p, ro, j] * LOG2E)
            acc = jnp.where(off == j, val, acc)
        tiles.append(jnp.where(valid, acc, NEG))
    for r0 in range(n_ro - WIN_ROWS + 1):
        for kr in range(WIN_ROWS):
            o_ref[r0, 0, kr * GRID_W:(kr + 1) * GRID_W, :] = tiles[r0 + kr]


def _rpb_tables(rpb):
    nh, n_ro, _ = rpb.shape
    ncase = n_ro - WIN_ROWS + 1
    shape = (ncase, nh // 2, WIN_ROWS * GRID_W, 2 * GRID_W)
    return pl.pallas_call(
        _rpb_expand_kernel,
        grid=(nh // 2,),
        in_specs=[pl.BlockSpec(memory_space=pltpu.SMEM)],
        out_specs=pl.BlockSpec((ncase, 1) + shape[2:], lambda p: (0, p, 0, 0)),
        out_shape=jax.ShapeDtypeStruct(shape, F32),
        compiler_params=_cparams(("arbitrary",)),
        name="rpb_expand",
    )(rpb)


def _attn_pair(qp, parts):
    return _attn_finish(_attn_scores(qp, [(k, bias) for k, _, bias in parts]),
                        [v for _, v, _ in parts], qp.shape[0])


def _attn_scores(qp, parts):
    lane = lax.broadcasted_iota(jnp.int32, qp.shape, 1)
    zero = jnp.zeros_like(qp)
    qbd = jnp.concatenate([jnp.where(lane < NA_HEAD_DIM, qp, zero),
                           jnp.where(lane >= NA_HEAD_DIM, qp, zero)], axis=0)
    qbd_t = jnp.transpose(qbd)
    scores = []
    for k, bias in parts:
        s = jnp.dot(k, qbd_t, preferred_element_type=F32)
        scores.append(s if bias is None else s + bias)
    return scores


def _attn_finish(scores, vs, nq):
    mx = functools.reduce(jnp.maximum, [jnp.max(s, axis=0, keepdims=True) for s in scores])
    es = [jnp.exp2(s - mx) for s in scores]
    den = functools.reduce(jnp.add, [jnp.sum(e, axis=0, keepdims=True) for e in es])
    tn = (((0,), (0,)), ((), ()))
    rt = functools.reduce(jnp.add, [
        lax.dot_general(v, e.astype(BF16), tn, preferred_element_type=F32)
        for e, v in zip(es, vs)])
    r = jnp.transpose(rt * (1.0 / den))
    lane_o = lax.broadcasted_iota(jnp.int32, (nq, LANES), 1)
    return jnp.where(lane_o < NA_HEAD_DIM, r[:nq], r[nq:])


def _na_attn_kernel(q_ref, k_ref, v_ref, kc_ref, vc_ref, bias_ref, o_ref, *, rows):
    r = pl.program_id(1)
    rs = jnp.clip(r - WIN_ROWS // 2, 0, rows - WIN_ROWS)
    nwin = WIN_ROWS * GRID_W // 2
    start = pl.multiple_of(rs * (GRID_W // 2), GRID_W // 2)
    npairs = q_ref.shape[1] // LANES
    nq = 2 * q_ref.shape[0]

    def scores_of(p):
        ls = slice(p * LANES, (p + 1) * LANES)
        return _attn_scores(_unpack_rows(q_ref[:, ls]),
                            [(_unpack_rows(k_ref[pl.ds(start, nwin), ls]), bias_ref[0, p]),
                             (_unpack_rows(kc_ref[:, ls]), None)])

    ready = [scores_of(p) for p in range(min(NA_LOOKAHEAD, npairs))]
    outs = []
    for p in range(npairs):
        ls = slice(p * LANES, (p + 1) * LANES)
        cur = ready.pop(0)
        if p + NA_LOOKAHEAD < npairs:
            ready.append(scores_of(p + NA_LOOKAHEAD))
        outs.append(_attn_finish(cur, [_unpack_rows(v_ref[pl.ds(start, nwin), ls]),
                                       _unpack_rows(vc_ref[:, ls])], nq))
    o_ref[...] = jnp.concatenate(outs, axis=1).astype(o_ref.dtype)


def _na_attn(q, k, v, kc, vc, bias_t):
    b, lw, d = q.shape
    l = 2 * lw
    ctw = kc.shape[1]
    rows = l // GRID_W
    half = WIN_ROWS // 2

    def bias_idx(bi, r):
        rs = jnp.clip(r - half, 0, rows - WIN_ROWS)
        return (rs - r + (WIN_ROWS - 1), 0, 0, 0)

    return pl.pallas_call(
        functools.partial(_na_attn_kernel, rows=rows),
        grid=(b, rows),
        in_specs=[
            pl.BlockSpec((None, GRID_W // 2, d), lambda bi, r: (bi, r, 0)),
            pl.BlockSpec((None, lw, d), lambda bi, r: (bi, 0, 0), pipeline_mode=pl.Buffered(1)),
            pl.BlockSpec((None, lw, d), lambda bi, r: (bi, 0, 0), pipeline_mode=pl.Buffered(1)),
            pl.BlockSpec((None, ctw, d), lambda bi, r: (bi, 0, 0)),
            pl.BlockSpec((None, ctw, d), lambda bi, r: (bi, 0, 0)),
            pl.BlockSpec((1,) + bias_t.shape[1:], bias_idx),
        ],
        out_specs=pl.BlockSpec((None, GRID_W, d), lambda bi, r: (bi, r, 0)),
        out_shape=jax.ShapeDtypeStruct((b, l, d), BF16),
        compiler_params=_cparams(("arbitrary", "arbitrary")),
        name="na_attn",
    )(q, k, v, kc, vc, bias_t)


def _ctx_attn_kernel(q_ref, k_ref, v_ref, o_ref):
    for p in range(q_ref.shape[2] // LANES):
        ls = slice(p * LANES, (p + 1) * LANES)
        o = _attn_pair(_unpack_rows(q_ref[0, :, ls]),
                       [(_unpack_rows(k_ref[0, :, ls]), _unpack_rows(v_ref[0, :, ls]), None)])
        o_ref[0, :, ls] = o.astype(o_ref.dtype)


def _ctx_attn(q, k, v):
    b, ctw, d = q.shape
    spec = pl.BlockSpec((1, ctw, d), lambda bi: (bi, 0, 0))
    return pl.pallas_call(
        _ctx_attn_kernel,
        grid=(b,),
        in_specs=[spec, spec, spec],
        out_specs=pl.BlockSpec((1, 2 * ctw, d), lambda bi: (bi, 0, 0)),
        out_shape=jax.ShapeDtypeStruct((b, 2 * ctw, d), BF16),
        compiler_params=_cparams(("arbitrary",)),
        name="ctx_attn",
    )(q, k, v)


def _prep_na(w_qkv, qg, kg):
    d = w_qkv.shape[0]
    nh = d // NA_HEAD_DIM
    head = jnp.arange(2 * LANES) // NA_HEAD_DIM
    ones_bd = (head[:, None] == head[None, :]).astype(BF16)
    qg_t = jnp.tile(qg, nh).reshape(1, d) * (NA_HEAD_DIM ** -0.5 * LOG2E)
    kg_t = jnp.tile(kg, nh).reshape(1, d)
    return w_qkv.astype(BF16), qg_t, kg_t, ones_bd


def _na_mixer(xc, xl, mc, ml, g, p, bias_t, need_ctx, tm):
    w_qkv, qg_t, kg_t, ones_bd = p
    q_c, k_c, v_c = _inproj_na(xc, mc, g, w_qkv, qg_t, kg_t, ones_bd, tm)
    q_l, k_l, v_l = _inproj_na(xl, ml, g, w_qkv, qg_t, kg_t, ones_bd, tm)
    o_l = _na_attn(q_l, k_l, v_l, k_c, v_c, bias_t)
    o_c = _ctx_attn(q_c, k_c, v_c) if need_ctx else None
    return o_c, o_l


HG_DK = 128
HG_CHUNK = 64
HG_SUB = 16
HG_HEADS_PER_STEP = 4
HG_SAFE_LOG2 = 100.0


def _lb_kernel(x_ref, o_ref):
    x = x_ref[...]
    e = jnp.exp(x - jnp.max(x, axis=0, keepdims=True))
    p = e / jnp.sum(e, axis=0, keepdims=True)
    acc = jnp.zeros_like(p[0:1])
    for l in range(x.shape[0]):
        o_ref[l:l + 1, :] = acc
        if l + 1 < x.shape[0]:
            acc = acc + p[l + 1:l + 2]


def _lower_bounds(lb_logits):
    return pl.pallas_call(
        _lb_kernel,
        out_shape=jax.ShapeDtypeStruct(lb_logits.shape, F32),
        name="hg_lower_bounds",
    )(lb_logits)


def _inproj_hg_kernel(x_ref, m_ref, g_ref, w_ref, q_ref, v_ref, gs_ref, zf_ref, zb_ref):
    d = x_ref.shape[1]
    h = _norm_mod(x_ref[...], g_ref[...], m_ref[:, 0:d], m_ref[:, d:2 * d])
    y = jnp.dot(h.astype(BF16), w_ref[...], preferred_element_type=F32)
    q_ref[...] = _silu(y[:, 0:d]).astype(BF16)
    v_ref[...] = y[:, d:2 * d].astype(BF16)
    gs_ref[...] = _silu(y[:, 2 * d:3 * d]).astype(BF16)
    zf_ref[...] = y[:, 3 * d:4 * d]
    zb_ref[...] = y[:, 4 * d:5 * d]


def _inproj_hg(x, m, g, w_in, tm):
    rows, d = x.shape
    spec = pl.BlockSpec((tm, d), lambda i: (i, 0))
    return pl.pallas_call(
        _inproj_hg_kernel,
        grid=(rows // tm,),
        in_specs=[spec, _const_spec((SUBLANES, 6 * d)), _const_spec((1, d)), _const_spec((d, 5 * d))],
        out_specs=[spec] * 5,
        out_shape=[jax.ShapeDtypeStruct((rows, d), BF16)] * 3 + [jax.ShapeDtypeStruct((rows, d), F32)] * 2,
        compiler_params=_cparams(("arbitrary",)),
        name="inproj_hg",
    )(x, m, g, w_in)


def _hg_scan_kernel(*refs, reverse, combine):
    nin = 8 if combine else 5
    if combine:
        q_ref, v_ref, z_ref, lb_ref, s0_ref, oo_ref, gs_ref, ng_ref = refs[:nin]
    else:
        q_ref, v_ref, z_ref, lb_ref, s0_ref = refs[:nin]
        oo_ref = gs_ref = ng_ref = None
    out_ref, sout_ref, s_scr = refs[nin:nin + 3]
    hp = q_ref.shape[1] // LANES
    nwork = 7
    work = [refs[nin + 3 + nwork * hh:nin + 3 + nwork * (hh + 1)] for hh in range(hp)]
    h = pl.program_id(0)
    i = pl.program_id(1)

    @pl.when(i == 0)
    def _():
        s_scr[...] = s0_ref[...]

    @pl.when((i == 0) & (h == 0))
    def _():
        for qe_scr, ke_scr, *_ in work:
            qe_scr[...] = jnp.zeros_like(qe_scr)
            ke_scr[...] = jnp.zeros_like(ke_scr)

    for hh in range(hp):
        def cols(ref):
            return None if ref is None else ref.at[:, hh * LANES:(hh + 1) * LANES]
        _hg_head(cols(q_ref), cols(v_ref), cols(z_ref), cols(lb_ref), cols(oo_ref), cols(gs_ref), ng_ref,
                 cols(out_ref), sout_ref.at[hh], s_scr.at[hh], *work[hh], reverse, combine)


def _hg_head(q_ref, v_ref, z_ref, lb_ref, oo_ref, gs_ref, ng_ref, out_ref, sout_ref, s_scr, qe_scr,
             ke_scr, qt_scr, kt_scr, kh_scr, v_scr, o_scr, reverse, combine):
    nsteps = HG_CHUNK
    nsub = nsteps // HG_SUB
    rows = nsteps * SUBLANES
    order = range(nsteps - 1, -1, -1) if reverse else range(nsteps)

    def load_steps(ref):
        x3 = ref[...].astype(F32).reshape(nsteps, SUBLANES, LANES)
        return jnp.stack([x3[t] for t in order]) if reverse else x3

    def store_steps(val3):
        return jnp.concatenate([val3[s] for s in order], axis=0)

    lb = lb_ref[...]
    q = load_steps(q_ref)
    v = load_steps(v_ref)
    sg = _sigmoid(load_steps(z_ref))
    k = (1.0 - lb) * (1.0 - sg)
    logf = jnp.log(lb + (1.0 - lb) * sg) * LOG2E
    acc = logf[0]
    cum = [acc]
    for s in range(1, nsteps):
        acc = acc + logf[s]
        cum.append(acc)
    b = jnp.stack(cum)
    b_end = cum[-1]

    v_scr[...] = v.reshape(rows, LANES)
    qt_scr[...] = (q * jnp.exp2(b)).reshape(rows, LANES)
    kt_scr[...] = (k * jnp.exp2(b_end[None] - b)).reshape(rows, LANES)

    nt = (((1,), (1,)), ((), ()))
    tn = (((0,), (0,)), ((), ()))
    decay_end = jnp.exp2(b_end)

    def batch_updates(a_intra):
        results = []
        for bb in range(SUBLANES):
            sl = pl.ds(bb, nsteps, stride=SUBLANES)
            vb = v_scr[sl, :].astype(BF16)
            st = s_scr[bb]
            o_b = (jnp.dot(a_intra[bb].astype(BF16), vb, preferred_element_type=F32)
                   + lax.dot_general(qt_scr[sl, :].astype(BF16), st.astype(BF16), nt,
                                     preferred_element_type=F32))
            s_new = st * decay_end[bb:bb + 1, :] + lax.dot_general(
                vb, kt_scr[sl, :].astype(BF16), tn, preferred_element_type=F32)
            results.append((o_b, s_new))
        for bb, (o_b, s_new) in enumerate(results):
            o_scr[pl.ds(bb, nsteps, stride=SUBLANES), :] = o_b
            s_scr[bb] = s_new
            sout_ref[bb] = s_new

    safe = jnp.min(b_end) > -HG_SAFE_LOG2

    @pl.when(safe)
    def _():
        kh_scr[...] = (k * jnp.exp2(-b)).reshape(rows, LANES)
        t_idx = lax.broadcasted_iota(jnp.int32, (nsteps, nsteps), 0)
        s_idx = lax.broadcasted_iota(jnp.int32, (nsteps, nsteps), 1)
        a_all = []
        for bb in range(SUBLANES):
            sl = pl.ds(bb, nsteps, stride=SUBLANES)
            a = lax.dot_general(qt_scr[sl, :].astype(BF16), kh_scr[sl, :].astype(BF16), nt,
                                preferred_element_type=F32)
            a_all.append(jnp.where(s_idx <= t_idx, a, 0.0))
        batch_updates(a_all)

    @pl.when(jnp.logical_not(safe))
    def _():
        for blk in range(1, nsub):
            lo = blk * HG_SUB
            beta = cum[lo - 1][None]
            qe_scr[blk - 1, lo * SUBLANES:(lo + HG_SUB) * SUBLANES, :] = (
                q[lo:lo + HG_SUB] * jnp.exp2(b[lo:lo + HG_SUB] - beta)).reshape(HG_SUB * SUBLANES, LANES)
            ke_scr[blk - 1, 0:lo * SUBLANES, :] = (
                k[:lo] * jnp.exp2(beta - b[:lo])).reshape(lo * SUBLANES, LANES)
        a_all = []
        for bb in range(SUBLANES):
            sl = pl.ds(bb, nsteps, stride=SUBLANES)
            qe = jnp.concatenate([qe_scr[j, sl, :] for j in range(nsub - 1)], axis=1).astype(BF16)
            ke = jnp.concatenate([ke_scr[j, sl, :] for j in range(nsub - 1)], axis=1).astype(BF16)
            a_all.append(lax.dot_general(qe, ke, nt, preferred_element_type=F32))
        batch_updates(a_all)
        shp = (nsub, HG_SUB, SUBLANES, LANES)
        b4, q4, k4, v4 = b.reshape(shp), q.reshape(shp), k.reshape(shp), v.reshape(shp)
        near0 = jnp.sum(q4 * k4, axis=-1, keepdims=True) * v4
        near = [near0[:, t] for t in range(HG_SUB)]
        for dlt in range(1, HG_SUB):
            n = HG_SUB - dlt
            w = q4[:, dlt:] * k4[:, :n] * jnp.exp2(b4[:, dlt:] - b4[:, :n])
            contrib = jnp.sum(w, axis=-1, keepdims=True) * v4[:, :n]
            for j in range(n):
                near[dlt + j] = near[dlt + j] + contrib[:, j]
        o_scr[...] = o_scr[...] + jnp.stack(near, axis=1).reshape(rows, LANES)

    o_tot = store_steps(o_scr[...].reshape(nsteps, SUBLANES, LANES))
    if combine:
        tot = oo_ref[...] + o_tot
        y = tot * lax.rsqrt(jnp.mean(tot * tot, axis=-1, keepdims=True) + EPS) * ng_ref[...]
        out_ref[...] = (y * gs_ref[...].astype(F32)).astype(out_ref.dtype)
    else:
        out_ref[...] = o_tot


def _hg_scan(q, v, z, lb, s0, *, reverse, o_other=None, gs=None, norm_g=None):
    rows_total, d = q.shape
    nh = d // HG_DK
    rows = HG_CHUNK * SUBLANES
    nblk = rows_total // rows
    combine = o_other is not None

    def blk(i):
        return (nblk - 1 - i) if reverse else i

    hp = HG_HEADS_PER_STEP
    tile = pl.BlockSpec((rows, hp * HG_DK), lambda h, i: (blk(i), h))
    state = pl.BlockSpec((hp, SUBLANES, HG_DK, HG_DK), lambda h, i: (h, 0, 0, 0))
    in_specs = [tile, tile, tile, pl.BlockSpec((1, hp * HG_DK), lambda h, i: (0, h)), state]
    args = [q, v, z, lb, s0]
    if combine:
        in_specs += [tile, tile, pl.BlockSpec((1, HG_DK), lambda h, i: (0, 0))]
        args += [o_other, gs, norm_g]
    nref = HG_CHUNK // HG_SUB - 1
    return pl.pallas_call(
        functools.partial(_hg_scan_kernel, reverse=reverse, combine=combine),
        grid=(nh // hp, nblk),
        in_specs=in_specs,
        out_specs=[tile, state],
        out_shape=[jax.ShapeDtypeStruct((rows_total, d), BF16 if combine else F32),
                   jax.ShapeDtypeStruct(s0.shape, F32)],
        scratch_shapes=[pltpu.VMEM((hp, SUBLANES, HG_DK, HG_DK), F32)] + hp * [
            pltpu.VMEM((nref, rows, LANES), F32),
            pltpu.VMEM((nref, rows, LANES), F32),
            pltpu.VMEM((rows, LANES), F32),
            pltpu.VMEM((rows, LANES), F32),
            pltpu.VMEM((rows, LANES), F32),
            pltpu.VMEM((rows, LANES), F32),
            pltpu.VMEM((rows, LANES), F32),
        ],
        compiler_params=_cparams(("arbitrary", "arbitrary")),
        name="hg_scan_bwd" if reverse else "hg_scan_fwd",
    )(*args)


def _hg_mixer(xc, xl, mc, ml, g, w_in, lb, norm_g, need_ctx, tm):
    d = xl.shape[1]
    nh = d // HG_DK
    q_c, v_c, gs_c, zf_c, zb_c = _inproj_hg(xc, mc, g, w_in, tm)
    q_l, v_l, gs_l, zf_l, zb_l = _inproj_hg(xl, ml, g, w_in, tm)
    s0 = jnp.zeros((nh, SUBLANES, HG_DK, HG_DK), F32)
    of_c, s_c = _hg_scan(q_c, v_c, zf_c, lb, s0, reverse=False)
    of_l, _ = _hg_scan(q_l, v_l, zf_l, lb, s_c, reverse=False)
    if need_ctx:
        z_c, s_c = _hg_scan(q_c, v_c, zb_c, lb, s0, reverse=True, o_other=of_c, gs=gs_c, norm_g=norm_g)
    else:
        z_c = None
        _, s_c = _hg_scan(q_c, v_c, zb_c, lb, s0, reverse=True)
    z_l, _ = _hg_scan(q_l, v_l, zb_l, lb, s_c, reverse=True, o_other=of_l, gs=gs_l, norm_g=norm_g)
    return z_c, z_l


def kernel(x, c, ctx, c_ctx, mod_w, mod_b, norm_mix_g, norm_ffn_g, ffn_w_gu, ffn_w_down, lru_w_in, lru_conv_w, lru_conv_b, lru_gate_w, lru_gate_b, lru_lambda, lru_w_out, na_w_qkv, na_q_norm_g, na_k_norm_g, na_rpb, na_w_o, hg_w_in, hg_lb_logits, hg_norm_g, hg_w_o):
    b, l, d = x.shape
    ct = ctx.shape[1]
    depth = mod_w.shape[0]
    tm = ROW_TILE
    cc = jnp.zeros((2 * SUBLANES, d), F32).at[:b].set(c).at[b].set(c_ctx)
    mod = _modulation(cc, mod_w, mod_b)
    lower_bounds = _lower_bounds(hg_lb_logits)
    xl, xc = x, ctx
    for layer in range(depth):
        kind, slot = layer % 3, layer // 3
        need_ctx = layer < depth - 1
        ml = mod[layer, :b]
        mc = jnp.broadcast_to(mod[layer, b:b + 1], (b, 6 * d))
        g = norm_mix_g[layer].reshape(1, d)
        gf = norm_ffn_g[layer].reshape(1, d)
        wgu = ffn_w_gu[layer].astype(BF16)
        wd = ffn_w_down[layer].astype(BF16)
        if kind == 0:
            p = _prep_lru(lru_w_in[slot], lru_conv_w[slot], lru_conv_b[slot], lru_gate_w[slot],
                          lru_gate_b[slot], lru_lambda[slot])
            zc, zl = _lru_mixer(xc, xl, mc, ml, g, p, need_ctx, tm, LRU_STEPS)
            wo = lru_w_out[slot].astype(BF16)
        elif kind == 1:
            p = _prep_na(na_w_qkv[slot], na_q_norm_g[slot], na_k_norm_g[slot])
            zc, zl = _na_mixer(xc, xl, mc, ml, g, p, _rpb_tables(na_rpb[slot]), need_ctx, tm)
            wo = na_w_o[slot].astype(BF16)
        else:
            zc, zl = _hg_mixer(xc, xl, mc, ml, g, hg_w_in[slot].astype(BF16),
                               lower_bounds[layer:layer + 1], hg_norm_g[slot].reshape(1, -1), need_ctx, tm)
            wo = hg_w_o[slot].astype(BF16)
        xl = _out_ffn(xl, zl, ml, gf, wo, wgu, wd, tm, out_per_batch=layer == depth - 1)
        if need_ctx:
            xc = _out_ffn(xc, zc, mc, gf, wo, wgu, wd, tm)
    return xl
```
